```python
import math
import jax, jax.numpy as jnp
from jax import lax
import numpy as np

D_MODEL = 1024
BATCH = 4
SEQ = 4096
DEPTH = 2

HEAD_DIM = 64
BLOCK = 128
WINDOW = 128
A_Q_HEADS = 8
A_KV_HEADS = 2
A_GROUP = A_Q_HEADS // A_KV_HEADS
B_HEADS = 4
B_V_DIM = 2 * HEAD_DIM
D_FF = 2816
RMS_EPS = 1e-6
NEG_INF = -1e30

A_Q_COLS = A_Q_HEADS * HEAD_DIM
A_KV_COLS = A_KV_HEADS * HEAD_DIM
B_QK_COLS = B_HEADS * 2 * HEAD_DIM
B_V_COLS = B_HEADS * B_V_DIM
IN_COLS = A_Q_COLS + 2 * A_KV_COLS + 2 * B_QK_COLS + B_V_COLS
MIX_WIDTH = A_Q_COLS + B_V_COLS
SPLITS = list(np.cumsum([A_Q_COLS, A_KV_COLS, A_KV_COLS, B_QK_COLS, B_QK_COLS]))

kernel_name = "hybrid_swa_sink_diffattn_alibi_macaron"


def alibi_slopes(n):
    return jnp.exp2(-8.0 * jnp.arange(1, n + 1, dtype=jnp.float32) / n)


def rms_norm(x, g):
    xf = x.astype(jnp.float32)
    y = xf * lax.rsqrt(jnp.mean(xf * xf, axis=-1, keepdims=True) + RMS_EPS)
    return (y * g.astype(jnp.float32)).astype(x.dtype)


def swiglu(h, w_gate, w_up, w_down):
    return (jax.nn.silu(h @ w_gate) * (h @ w_up)) @ w_down


def windowed_gqa_sink(q, k, v, sink, slopes):
    b, s, _, dh = q.shape
    nb = s // BLOCK
    qb = q.reshape(b, nb, BLOCK, A_KV_HEADS, A_GROUP, dh)
    pad = ((0, 0), (BLOCK, BLOCK), (0, 0), (0, 0))
    kp = jnp.pad(k, pad)
    vp = jnp.pad(v, pad)
    key_idx = jnp.arange(nb)[:, None] * BLOCK + jnp.arange(3 * BLOCK)[None, :]
    kb = kp[:, key_idx]
    vb = vp[:, key_idx]
    scores = jnp.einsum('bnqkgd,bnjkd->bnkgqj', qb, kb).astype(jnp.float32) * (dh ** -0.5)
    q_pos = jnp.arange(nb)[:, None] * BLOCK + jnp.arange(BLOCK)[None, :]
    k_pos = key_idx - BLOCK
    dist = jnp.abs(q_pos[:, :, None] - k_pos[:, None, :])
    valid = (dist <= WINDOW) & (k_pos[:, None, :] >= 0) & (k_pos[:, None, :] < s)
    bias = -slopes.reshape(A_KV_HEADS, A_GROUP)[None, :, :, None, None] * dist[:, None, None].astype(jnp.float32)
    scores = jnp.where(valid[:, None, None], scores + bias, NEG_INF)
    sink_l = sink.astype(jnp.float32).reshape(1, 1, A_KV_HEADS, A_GROUP, 1, 1)
    m = jnp.maximum(jnp.max(scores, axis=-1, keepdims=True), sink_l)
    e = jnp.exp(scores - m)
    p = e / (jnp.sum(e, axis=-1, keepdims=True) + jnp.exp(sink_l - m))
    out = jnp.einsum('bnkgqj,bnjkd->bnqkgd', p.astype(v.dtype), vb)
    return out.reshape(b, s, A_Q_HEADS * dh)


def differential_attention(q, k, v, lam, slopes):
    b, s, h, _, dh = q.shape
    nb = s // BLOCK
    q_blocks = jnp.moveaxis(q.reshape(b, nb, BLOCK, h, 2, dh), 1, 0)
    starts = jnp.arange(nb) * BLOCK
    key_pos = jnp.arange(s)
    scale = dh ** -0.5

    def one_block(args):
        qi, start = args
        sc = jnp.einsum('bqhcd,bkhcd->bhcqk', qi, k).astype(jnp.float32) * scale
        dist = jnp.abs((start + jnp.arange(BLOCK))[:, None] - key_pos[None, :]).astype(jnp.float32)
        sc = sc - slopes[:, None, None, None] * dist
        p = jax.nn.softmax(sc, axis=-1)
        a = p[:, :, 0] - lam * p[:, :, 1]
        return jnp.einsum('bhqk,bkhe->bqhe', a.astype(v.dtype), v)

    out = lax.map(one_block, (q_blocks, starts))
    return jnp.moveaxis(out, 0, 1).reshape(b, s, h, -1)


def setup_inputs(seed: int = 0) -> dict:
    key = jax.random.key(seed)
    ks = jax.random.split(key, 24)
    f32 = jnp.float32

    def nrm(k, shape, scale):
        return jax.random.normal(k, shape, f32) * scale

    def gain(k, shape):
        return 1.0 + 0.02 * jax.random.normal(k, shape, f32)

    return {
        "x": jax.random.normal(ks[0], (BATCH, SEQ, D_MODEL), f32),
        "ffn1_norm": gain(ks[1], (DEPTH, D_MODEL)),
        "ffn1_w_gate": nrm(ks[2], (DEPTH, D_MODEL, D_FF), D_MODEL ** -0.5),
        "ffn1_w_up": nrm(ks[3], (DEPTH, D_MODEL, D_FF), D_MODEL ** -0.5),
        "ffn1_w_down": nrm(ks[4], (DEPTH, D_FF, D_MODEL), D_FF ** -0.5),
        "mix_norm": gain(ks[5], (DEPTH, D_MODEL)),
        "w_in": nrm(ks[6], (DEPTH, D_MODEL, IN_COLS), D_MODEL ** -0.5),
        "sink": nrm(ks[7], (DEPTH, A_Q_HEADS), 0.5),
        "lam_q1": nrm(ks[8], (DEPTH, HEAD_DIM), 0.1),
        "lam_k1": nrm(ks[9], (DEPTH, HEAD_DIM), 0.1),
        "lam_q2": nrm(ks[10], (DEPTH, HEAD_DIM), 0.1),
        "lam_k2": nrm(ks[11], (DEPTH, HEAD_DIM), 0.1),
        "diff_subln": gain(ks[12], (DEPTH, B_V_DIM)),
        "w_out": nrm(ks[13], (DEPTH, MIX_WIDTH, D_MODEL), MIX_WIDTH ** -0.5),
        "ffn2_norm": gain(ks[14], (DEPTH, D_MODEL)),
        "ffn2_w_gate": nrm(ks[15], (DEPTH, D_MODEL, D_FF), D_MODEL ** -0.5),
        "ffn2_w_up": nrm(ks[16], (DEPTH, D_MODEL, D_FF), D_MODEL ** -0.5),
        "ffn2_w_down": nrm(ks[17], (DEPTH, D_FF, D_MODEL), D_FF ** -0.5),
        "final_norm": gain(ks[18], (D_MODEL,)),
    }


def reference(x, ffn1_norm, ffn1_w_gate, ffn1_w_up, ffn1_w_down, mix_norm, w_in, sink,
              lam_q1, lam_k1, lam_q2, lam_k2, diff_subln, w_out,
              ffn2_norm, ffn2_w_gate, ffn2_w_up, ffn2_w_down, final_norm):
    b, s, _ = x.shape
    slopes_a = alibi_slopes(A_Q_HEADS)
    slopes_b = alibi_slopes(B_HEADS)
    for l in range(DEPTH):
        x = x + 0.5 * swiglu(rms_norm(x, ffn1_norm[l]), ffn1_w_gate[l], ffn1_w_up[l], ffn1_w_down[l])

        h = rms_norm(x, mix_norm[l])
        proj = h @ w_in[l]
        qa, ka, va, qb, kb, vb = jnp.split(proj, SPLITS, axis=-1)
        qa = qa.reshape(b, s, A_Q_HEADS, HEAD_DIM)
        ka = ka.reshape(b, s, A_KV_HEADS, HEAD_DIM)
        va = va.reshape(b, s, A_KV_HEADS, HEAD_DIM)
        qb = qb.reshape(b, s, B_HEADS, 2, HEAD_DIM)
        kb = kb.reshape(b, s, B_HEADS, 2, HEAD_DIM)
        vb = vb.reshape(b, s, B_HEADS, B_V_DIM)

        out_a = windowed_gqa_sink(qa, ka, va, sink[l], slopes_a)

        lam_init = 0.8 - 0.6 * math.exp(-0.3 * l)
        lam = (jnp.exp(jnp.sum(lam_q1[l].astype(jnp.float32) * lam_k1[l].astype(jnp.float32)))
               - jnp.exp(jnp.sum(lam_q2[l].astype(jnp.float32) * lam_k2[l].astype(jnp.float32)))
               + lam_init)
        out_b = differential_attention(qb, kb, vb, lam, slopes_b)
        out_b = (rms_norm(out_b, diff_subln[l]) * (1.0 - lam_init)).reshape(b, s, B_V_COLS)

        x = x + jnp.concatenate([out_a, out_b], axis=-1) @ w_out[l]

        x = x + 0.5 * swiglu(rms_norm(x, ffn2_norm[l]), ffn2_w_gate[l], ffn2_w_up[l], ffn2_w_down[l])
    return rms_norm(x, final_norm)
```

```python
import functools
import math

import jax
import jax.numpy as jnp
from jax import lax
from jax.experimental import pallas as pl
from jax.experimental.pallas import tpu as pltpu

F32 = jnp.float32
BF16 = jnp.bfloat16

D_MODEL = 1024
D_FF = 2816
HEAD_DIM = 64
WINDOW = 128
A_Q_HEADS = 8
A_KV_HEADS = 2
A_GROUP = A_Q_HEADS // A_KV_HEADS
B_HEADS = 4
RMS_EPS = 1e-6
NEG_INF = -1e30

LANES = 128
MXU_DIM = 256
A_Q_COLS = A_Q_HEADS * HEAD_DIM
B_COLS = B_HEADS * LANES
A_KV_DUP_COLS = 2 * A_KV_HEADS * HEAD_DIM
IN_COLS_DUP = A_Q_COLS + 2 * A_KV_DUP_COLS + 3 * B_COLS
MIX_WIDTH = A_Q_COLS + B_COLS

FF_CHUNK = MXU_DIM
TM_FFN = 512
TQ_ATTN = 256
A_BLOCK = 128
A_KEYS = 3 * A_BLOCK
SOFTMAX_ROWS = 32
VMEM_LIMIT = 56 * 1024 * 1024

SLOPES_A = tuple(2.0 ** (-8.0 * i / A_Q_HEADS) for i in range(1, A_Q_HEADS + 1))
SLOPES_B = tuple(2.0 ** (-8.0 * i / B_HEADS) for i in range(1, B_HEADS + 1))


def _rms(x, g):
    ms = jnp.mean(x * x, axis=-1, keepdims=True)
    return x * lax.rsqrt(ms + RMS_EPS) * g


def _dot(a, b):
    return jnp.dot(a, b, preferred_element_type=F32)


def _dot_nt(a, b):
    return lax.dot_general(a, b, (((1,), (1,)), ((), ())), preferred_element_type=F32)


def _half_masks(dtype):
    lane = lax.broadcasted_iota(jnp.int32, (1, LANES), 1)
    lo = (lane < HEAD_DIM).astype(F32).astype(dtype)
    hi = (lane >= HEAD_DIM).astype(F32).astype(dtype)
    return lo, hi


def _swiglu_residual(x, g_ref, wg_ref, wu_ref, wd_ref, h_scr):
    xn = _rms(x, g_ref[...]).astype(BF16)
    for c in range(D_FF // FF_CHUNK):
        sl = slice(c * FF_CHUNK, (c + 1) * FF_CHUNK)
        gate = _dot(xn, wg_ref[:, sl])
        up = _dot(xn, wu_ref[:, sl])
        h_scr[:, sl] = (gate * jax.nn.sigmoid(gate) * up).astype(BF16)
    return x + 0.5 * _dot(h_scr[...], wd_ref[...])


def _ffn_inproj_kernel(x_ref, g_ref, wg_ref, wu_ref, wd_ref, mg_ref, win_ref,
                       xo_ref, qa_ref, ka_ref, va_ref, qb_ref, kb_ref, vb_ref, h_scr):
    x1 = _swiglu_residual(x_ref[...], g_ref, wg_ref, wu_ref, wd_ref, h_scr)
    xo_ref[...] = x1
    hn = _rms(x1, mg_ref[...]).astype(BF16)
    proj = _dot(hn, win_ref[...]).astype(BF16)
    c0 = A_Q_COLS
    c1 = c0 + A_KV_DUP_COLS
    c2 = c1 + A_KV_DUP_COLS
    qa_ref[...] = proj[:, :c0]
    ka_ref[...] = proj[:, c0:c1]
    va_ref[...] = proj[:, c1:c2]
    for h in range(B_HEADS):
        qb_ref[h] = proj[:, c2 + h * LANES: c2 + (h + 1) * LANES]
        kb_ref[h] = proj[:, c2 + B_COLS + h * LANES: c2 + B_COLS + (h + 1) * LANES]
        vb_ref[h] = proj[:, c2 + 2 * B_COLS + h * LANES: c2 + 2 * B_COLS + (h + 1) * LANES]


def _ffn_kernel(x_ref, g_ref, wg_ref, wu_ref, wd_ref, fg_ref, xo_ref, h_scr, *, final_norm):
    x1 = _swiglu_residual(x_ref[...], g_ref, wg_ref, wu_ref, wd_ref, h_scr)
    if final_norm:
        x1 = _rms(x1, fg_ref[...])
    xo_ref[...] = x1


def _resident(shape):
    nd = len(shape)
    return pl.BlockSpec(shape, lambda *_: (0,) * nd, pipeline_mode=pl.Buffered(1))


def _ffn_weight_specs():
    return [_resident((1, D_MODEL)), _resident((D_MODEL, D_FF)), _resident((D_MODEL, D_FF)),
            _resident((D_FF, D_MODEL))]


def _ffn_inproj_call(x, g, wg, wu, wd, mg, win):
    n = x.shape[0]
    tm = TM_FFN
    row = lambda cols: pl.BlockSpec((tm, cols), lambda i: (i, 0))
    headrow = pl.BlockSpec((B_HEADS, tm, LANES), lambda i: (0, i, 0))
    out_shape = (
        jax.ShapeDtypeStruct((n, D_MODEL), F32),
        jax.ShapeDtypeStruct((n, A_Q_COLS), BF16),
        jax.ShapeDtypeStruct((n, A_KV_DUP_COLS), BF16),
        jax.ShapeDtypeStruct((n, A_KV_DUP_COLS), BF16),
        jax.ShapeDtypeStruct((B_HEADS, n, LANES), BF16),
        jax.ShapeDtypeStruct((B_HEADS, n, LANES), BF16),
        jax.ShapeDtypeStruct((B_HEADS, n, LANES), BF16),
    )
    return pl.pallas_call(
        _ffn_inproj_kernel,
        grid=(n // tm,),
        in_specs=[row(D_MODEL)] + _ffn_weight_specs()
                 + [_resident((1, D_MODEL)), _resident((D_MODEL, IN_COLS_DUP))],
        out_specs=(row(D_MODEL), row(A_Q_COLS), row(A_KV_DUP_COLS), row(A_KV_DUP_COLS),
                   headrow, headrow, headrow),
        out_shape=out_shape,
        scratch_shapes=[pltpu.VMEM((tm, D_FF), BF16)],
        compiler_params=pltpu.CompilerParams(
            dimension_semantics=("arbitrary",), vmem_limit_bytes=VMEM_LIMIT),
        name="ffn_inproj",
    )(x, g, wg, wu, wd, mg, win)


def _ffn_call(x, g, wg, wu, wd, fg, final_norm):
    n = x.shape[0]
    tm = TM_FFN
    row = pl.BlockSpec((tm, D_MODEL), lambda i: (i, 0))
    return pl.pallas_call(
        functools.partial(_ffn_kernel, final_norm=final_norm),
        grid=(n // tm,),
        in_specs=[row] + _ffn_weight_specs() + [_resident((1, D_MODEL))],
        out_specs=row,
        out_shape=jax.ShapeDtypeStruct((n, D_MODEL), F32),
        scratch_shapes=[pltpu.VMEM((tm, D_FF), BF16)],
        compiler_params=pltpu.CompilerParams(
            dimension_semantics=("arbitrary",), vmem_limit_bytes=VMEM_LIMIT),
        name="ffn",
    )(x, g, wg, wu, wd, fg)


def _windowed_gqa(q0, qa_ref, ka_ref, va_ref, sink_ref, seq):
    tq = qa_ref.shape[0]
    mlo, mhi = _half_masks(BF16)
    lane = lax.broadcasted_iota(jnp.int32, (A_BLOCK, LANES), 1)
    a_idx = lax.broadcasted_iota(jnp.int32, (A_BLOCK, A_KEYS), 0)
    b_idx = lax.broadcasted_iota(jnp.int32, (A_BLOCK, A_KEYS), 1)
    row_blocks = []
    for r in range(tq // A_BLOCK):
        qs = q0 + r * A_BLOCK
        ws = pl.multiple_of(jnp.clip(qs - A_BLOCK, 0, seq - A_KEYS), A_BLOCK)
        dist = jnp.abs((qs - ws) + a_idx - b_idx)
        valid = dist <= WINDOW
        distf = dist.astype(F32)
        slabs = []
        for kv in range(A_KV_HEADS):
            kwin = ka_ref[pl.ds(ws, A_KEYS), kv * LANES:(kv + 1) * LANES]
            vwin = va_ref[pl.ds(ws, A_KEYS), kv * LANES:(kv + 1) * LANES]
            parts = []
            for g in range(A_GROUP):
                h = kv * A_GROUP + g
                slab = qa_ref[r * A_BLOCK:(r + 1) * A_BLOCK, (h // 2) * LANES:(h // 2 + 1) * LANES]
                parts.append(slab * (mlo if h % 2 == 0 else mhi))
            s = _dot_nt(jnp.concatenate(parts, axis=0), kwin)
            es, dens = [], []
            for g in range(A_GROUP):
                h = kv * A_GROUP + g
                sg = s[g * A_BLOCK:(g + 1) * A_BLOCK]
                sg = jnp.where(valid, sg - SLOPES_A[h] * distf, NEG_INF)
                sk = sink_ref[h]
                m = jnp.maximum(jnp.max(sg, axis=-1, keepdims=True), sk)
                e = jnp.exp(sg - m)
                dens.append(jnp.sum(e, axis=-1, keepdims=True) + jnp.exp(sk - m))
                es.append(e.astype(BF16))
            pv = _dot(jnp.concatenate(es, axis=0), vwin)
            outs = [pv[g * A_BLOCK:(g + 1) * A_BLOCK] / dens[g] for g in range(A_GROUP)]
            slabs.append(jnp.where(lane < HEAD_DIM, outs[0], outs[1]))
            slabs.append(jnp.where(lane < HEAD_DIM, outs[2], outs[3]))
        row_blocks.append(jnp.concatenate(slabs, axis=1))
    return jnp.concatenate(row_blocks, axis=0)


def _attn_kernel(x_ref, qa_ref, qb_ref, qaug_ref, ka_ref, va_ref, kb_ref, vb_ref, kaug_ref,
                 wout_ref, sink_ref, slopeb_ref, lq1_ref, lk1_ref, lq2_ref, lk2_ref, subln_ref,
                 xo_ref, s_scr, p_scr, ksgn_scr, fd_scr, ob_scr, *, lam_init):
    tq = x_ref.shape[0]
    seq = kb_ref.shape[1]
    q0 = pl.program_id(1) * tq

    lam = (jnp.exp(jnp.sum(lq1_ref[...] * lk1_ref[...], axis=-1, keepdims=True))
           - jnp.exp(jnp.sum(lq2_ref[...] * lk2_ref[...], axis=-1, keepdims=True)) + lam_init)

    kpos = lax.broadcasted_iota(jnp.int32, (seq, LANES), 0)
    kaug = kaug_ref[...]
    ksgn_scr[...] = jnp.where(kpos < q0 + tq, kaug, -kaug).astype(BF16)
    r_idx = lax.broadcasted_iota(jnp.int32, (2 * tq, tq), 0)
    c_idx = lax.broadcasted_iota(jnp.int32, (2 * tq, tq), 1)
    a_idx = jnp.where(r_idx >= tq, r_idx - tq, r_idx)
    fd_scr[...] = (2 * jnp.maximum(c_idx - a_idx, 0)).astype(F32)

    mlo, mhi = _half_masks(BF16)
    ones = jnp.ones((seq, LANES), BF16)
    diag = pl.ds(pl.multiple_of(q0, tq), tq)

    def head_body(h, carry):
        q = qb_ref[h]
        aq = qaug_ref[h]
        lhs = jnp.concatenate(
            [jnp.concatenate([q * mlo, q * mhi], axis=0), jnp.concatenate([aq, aq], axis=0)], axis=1)
        rhs = jnp.concatenate([kb_ref[h], ksgn_scr[...]], axis=1)
        s_scr[...] = _dot_nt(lhs, rhs)
        s_scr[:, diag] = s_scr[:, diag] - slopeb_ref[h] * fd_scr[...]

        def softmax_rows(r, c):
            rows = pl.ds(pl.multiple_of(r * SOFTMAX_ROWS, SOFTMAX_ROWS), SOFTMAX_ROWS)
            s = s_scr[rows, :]
            m = jnp.max(s, axis=-1, keepdims=True)
            p_scr[rows, :] = jnp.exp(s - m).astype(BF16)
            return c
        lax.fori_loop(0, 2 * tq // SOFTMAX_ROWS, softmax_rows, 0)

        ol = _dot(p_scr[...], jnp.concatenate([vb_ref[h], ones], axis=1))
        o1 = ol[:tq, :LANES] / ol[:tq, LANES:]
        o2 = ol[tq:, :LANES] / ol[tq:, LANES:]
        o = o1 - lam * o2
        ob_scr[h] = (_rms(o, subln_ref[...]) * (1.0 - lam_init)).astype(BF16)
        return carry
    lax.fori_loop(0, B_HEADS, head_body, 0)

    oa = _windowed_gqa(q0, qa_ref, ka_ref, va_ref, sink_ref, seq).astype(BF16)

    mix = jnp.concatenate([oa] + [ob_scr[h] for h in range(B_HEADS)], axis=1)
    xo_ref[...] = x_ref[...] + _dot(mix, wout_ref[...])


def _attn_call(x, qa, ka, va, qb, kb, vb, qaug, kaug, wout, sink, slopes_b,
               lq1, lk1, lq2, lk2, subln, batch, seq, lam_init):
    n = x.shape[0]
    tq = TQ_ATTN
    nq = seq // tq
    tile = lambda cols: pl.BlockSpec((tq, cols), lambda b, i: (b * nq + i, 0))
    per_batch = lambda cols: pl.BlockSpec((seq, cols), lambda b, i: (b, 0), pipeline_mode=pl.Buffered(1))
    head_tile = pl.BlockSpec((B_HEADS, tq, LANES), lambda b, i: (0, b * nq + i, 0))
    head_batch = pl.BlockSpec((B_HEADS, seq, LANES), lambda b, i: (0, b, 0), pipeline_mode=pl.Buffered(1))
    smem = pl.BlockSpec(memory_space=pltpu.SMEM)
    small = lambda cols: _resident((1, cols))
    return pl.pallas_call(
        functools.partial(_attn_kernel, lam_init=lam_init),
        grid=(batch, nq),
        in_specs=[tile(D_MODEL), tile(A_Q_COLS), head_tile,
                  pl.BlockSpec((B_HEADS, tq, LANES), lambda b, i: (0, i, 0)),
                  per_batch(A_KV_DUP_COLS), per_batch(A_KV_DUP_COLS), head_batch, head_batch,
                  _resident((seq, LANES)), _resident((MIX_WIDTH, D_MODEL)), smem, smem,
                  small(HEAD_DIM), small(HEAD_DIM), small(HEAD_DIM), small(HEAD_DIM), small(LANES)],
        out_specs=tile(D_MODEL),
        out_shape=jax.ShapeDtypeStruct((n, D_MODEL), F32),
        scratch_shapes=[pltpu.VMEM((2 * tq, seq), F32), pltpu.VMEM((2 * tq, seq), BF16),
                        pltpu.VMEM((seq, LANES), BF16), pltpu.VMEM((2 * tq, tq), F32),
                        pltpu.VMEM((B_HEADS, tq, LANES), BF16)],
        compiler_params=pltpu.CompilerParams(
            dimension_semantics=("arbitrary", "arbitrary"), vmem_limit_bytes=VMEM_LIMIT),
        name="attn",
    )(x, qa, qb, qaug, ka, va, kb, vb, kaug, wout, sink, slopes_b, lq1, lk1, lq2, lk2, subln)


def _position_tables(seq):
    pos = jnp.arange(seq, dtype=jnp.int32)
    lo = (pos % LANES).astype(F32)
    hi = (LANES * (pos // LANES)).astype(F32)
    one = jnp.ones((seq,), F32)
    pad = jnp.zeros((seq, LANES - 4), F32)
    kaug = jnp.concatenate([jnp.stack([lo, hi, one, one], axis=1), pad], axis=1)
    slopes = jnp.asarray(SLOPES_B, F32)[:, None, None]
    qcols = jnp.stack([one, one, -lo, -hi], axis=1)[None]
    qaug = jnp.concatenate([slopes * qcols, jnp.zeros((B_HEADS, seq, LANES - 4), F32)], axis=2)
    return qaug.astype(BF16), kaug


def _prep_w_in(w):
    scale = HEAD_DIM ** -0.5
    c = 0
    qa = w[:, c:c + A_Q_COLS] * scale; c += A_Q_COLS
    ka = w[:, c:c + A_KV_HEADS * HEAD_DIM]; c += A_KV_HEADS * HEAD_DIM
    va = w[:, c:c + A_KV_HEADS * HEAD_DIM]; c += A_KV_HEADS * HEAD_DIM
    qb = w[:, c:c + B_COLS] * scale; c += B_COLS
    kb = w[:, c:c + B_COLS]; c += B_COLS
    vb = w[:, c:c + B_COLS]
    dup = lambda t: jnp.concatenate(
        [t[:, j * HEAD_DIM:(j + 1) * HEAD_DIM] for j in range(A_KV_HEADS) for _ in range(2)], axis=1)
    return jnp.concatenate([qa, dup(ka), dup(va), qb, kb, vb], axis=1).astype(BF16)


def kernel(x, ffn1_norm, ffn1_w_gate, ffn1_w_up, ffn1_w_down, mix_norm, w_in, sink,
           lam_q1, lam_k1, lam_q2, lam_k2, diff_subln, w_out,
           ffn2_norm, ffn2_w_gate, ffn2_w_up, ffn2_w_down, final_norm):
    batch, seq, d = x.shape
    depth = w_in.shape[0]
    assert d == D_MODEL and seq % TQ_ATTN == 0 and (batch * seq) % TM_FFN == 0
    xf = x.reshape(batch * seq, d)
    qaug, kaug = _position_tables(seq)
    slopes_b = jnp.asarray(SLOPES_B, F32)
    row = lambda v: v.reshape(1, -1).astype(F32)
    for l in range(depth):
        lam_init = 0.8 - 0.6 * math.exp(-0.3 * l)
        xf, qa, ka, va, qb, kb, vb = _ffn_inproj_call(
            xf, row(ffn1_norm[l]), ffn1_w_gate[l].astype(BF16), ffn1_w_up[l].astype(BF16),
            ffn1_w_down[l].astype(BF16), row(mix_norm[l]), _prep_w_in(w_in[l]))
        xf = _attn_call(xf, qa, ka, va, qb, kb, vb, qaug, kaug, w_out[l].astype(BF16),
                        sink[l].astype(F32), slopes_b, row(lam_q1[l]), row(lam_k1[l]),
                        row(lam_q2[l]), row(lam_k2[l]), row(diff_subln[l]), batch, seq, lam_init)
        xf = _ffn_call(xf, row(ffn2_norm[l]), ffn2_w_gate[l].astype(BF16), ffn2_w_up[l].astype(BF16),
                       ffn2_w_down[l].astype(BF16), row(final_norm), final_norm=(l == depth - 1))
    return xf.reshape(batch, seq, d)
```

```python
import functools
import math

import jax
import jax.numpy as jnp
from jax import lax
from jax.experimental import pallas as pl
from jax.experimental.pallas import tpu as pltpu

F32 = jnp.float32
BF16 = jnp.bfloat16

D_MODEL = 1024
D_FF = 2816
HEAD_DIM = 64
WINDOW = 128
A_Q_HEADS = 8
A_KV_HEADS = 2
A_GROUP = A_Q_HEADS // A_KV_HEADS
B_HEADS = 4
RMS_EPS = 1e-6
NEG_INF = -1e30

LANES = 128
MXU_DIM = 256
A_Q_COLS = A_Q_HEADS * HEAD_DIM
B_COLS = B_HEADS * LANES
A_KV_DUP_COLS = 2 * A_KV_HEADS * HEAD_DIM
IN_COLS_DUP = A_Q_COLS + 2 * A_KV_DUP_COLS + 3 * B_COLS
MIX_WIDTH = A_Q_COLS + B_COLS

FF_CHUNK = MXU_DIM
TM_FFN = 512
TQ_ATTN = 256
A_BLOCK = 128
A_KEYS = 3 * A_BLOCK
SOFTMAX_ROWS = 32
VMEM_LIMIT = 56 * 1024 * 1024

SLOPES_A = tuple(2.0 ** (-8.0 * i / A_Q_HEADS) for i in range(1, A_Q_HEADS + 1))
SLOPES_B = tuple(2.0 ** (-8.0 * i / B_HEADS) for i in range(1, B_HEADS + 1))


def _rms(x, g):
    ms = jnp.mean(x * x, axis=-1, keepdims=True)
    return x * lax.rsqrt(ms + RMS_EPS) * g


def _dot(a, b):
    return jnp.dot(a, b, preferred_element_type=F32)


def _dot_nt(a, b):
    return lax.dot_general(a, b, (((1,), (1,)), ((), ())), preferred_element_type=F32)


def _half_masks(dtype):
    lane = lax.broadcasted_iota(jnp.int32, (1, LANES), 1)
    lo = (lane < HEAD_DIM).astype(F32).astype(dtype)
    hi = (lane >= HEAD_DIM).astype(F32).astype(dtype)
    return lo, hi


def _swiglu_residual(x, g_ref, wg_ref, wu_ref, wd_ref, h_scr):
    xn = _rms(x, g_ref[...]).astype(BF16)
    for c in range(D_FF // FF_CHUNK):
        sl = slice(c * FF_CHUNK, (c + 1) * FF_CHUNK)
        gate = _dot(xn, wg_ref[:, sl])
        up = _dot(xn, wu_ref[:, sl])
        h_scr[:, sl] = (gate * jax.nn.sigmoid(gate) * up).astype(BF16)
    return x + 0.5 * _dot(h_scr[...], wd_ref[...])


def _ffn_inproj_kernel(x_ref, g_ref, wg_ref, wu_ref, wd_ref, mg_ref, win_ref,
                       xo_ref, qa_ref, ka_ref, va_ref, qb_ref, kb_ref, vb_ref, h_scr):
    x1 = _swiglu_residual(x_ref[...], g_ref, wg_ref, wu_ref, wd_ref, h_scr)
    xo_ref[...] = x1
    hn = _rms(x1, mg_ref[...]).astype(BF16)
    proj = _dot(hn, win_ref[...]).astype(BF16)
    c0 = A_Q_COLS
    c1 = c0 + A_KV_DUP_COLS
    c2 = c1 + A_KV_DUP_COLS
    qa_ref[...] = proj[:, :c0]
    ka_ref[...] = proj[:, c0:c1]
    va_ref[...] = proj[:, c1:c2]
    for h in range(B_HEADS):
        qb_ref[h] = proj[:, c2 + h * LANES: c2 + (h + 1) * LANES]
        kb_ref[h] = proj[:, c2 + B_COLS + h * LANES: c2 + B_COLS + (h + 1) * LANES]
        vb_ref[h] = proj[:, c2 + 2 * B_COLS + h * LANES: c2 + 2 * B_COLS + (h + 1) * LANES]


def _ffn_kernel(x_ref, g_ref, wg_ref, wu_ref, wd_ref, fg_ref, xo_ref, h_scr, *, final_norm):
    x1 = _swiglu_residual(x_ref[...], g_ref, wg_ref, wu_ref, wd_ref, h_scr)
    if final_norm:
        x1 = _rms(x1, fg_ref[...])
    xo_ref[...] = x1


def _resident(shape):
    nd = len(shape)
    return pl.BlockSpec(shape, lambda *_: (0,) * nd, pipeline_mode=pl.Buffered(1))


def _ffn_weight_specs():
    return [_resident((1, D_MODEL)), _resident((D_MODEL, D_FF)), _resident((D_MODEL, D_FF)),
            _resident((D_FF, D_MODEL))]


def _ffn_inproj_call(x, g, wg, wu, wd, mg, win):
    n = x.shape[0]
    tm = TM_FFN
    row = lambda cols: pl.BlockSpec((tm, cols), lambda i: (i, 0))
    headrow = pl.BlockSpec((B_HEADS, tm, LANES), lambda i: (0, i, 0))
    out_shape = (
        jax.ShapeDtypeStruct((n, D_MODEL), F32),
        jax.ShapeDtypeStruct((n, A_Q_COLS), BF16),
        jax.ShapeDtypeStruct((n, A_KV_DUP_COLS), BF16),
        jax.ShapeDtypeStruct((n, A_KV_DUP_COLS), BF16),
        jax.ShapeDtypeStruct((B_HEADS, n, LANES), BF16),
        jax.ShapeDtypeStruct((B_HEADS, n, LANES), BF16),
        jax.ShapeDtypeStruct((B_HEADS, n, LANES), BF16),
    )
    return pl.pallas_call(
        _ffn_inproj_kernel,
        grid=(n // tm,),
        in_specs=[row(D_MODEL)] + _ffn_weight_specs()
                 + [_resident((1, D_MODEL)), _resident((D_MODEL, IN_COLS_DUP))],
        out_specs=(row(D_MODEL), row(A_Q_COLS), row(A_KV_DUP_COLS), row(A_KV_DUP_COLS),
                   headrow, headrow, headrow),
        out_shape=out_shape,
        scratch_shapes=[pltpu.VMEM((tm, D_FF), BF16)],
        compiler_params=pltpu.CompilerParams(
            dimension_semantics=("arbitrary",), vmem_limit_bytes=VMEM_LIMIT),
        name="ffn_inproj",
    )(x, g, wg, wu, wd, mg, win)


def _ffn_call(x, g, wg, wu, wd, fg, final_norm):
    n = x.shape[0]
    tm = TM_FFN
    row = pl.BlockSpec((tm, D_MODEL), lambda i: (i, 0))
    return pl.pallas_call(
        functools.partial(_ffn_kernel, final_norm=final_norm),
        grid=(n // tm,),
        in_specs=[row] + _ffn_weight_specs() + [_resident((1, D_MODEL))],
        out_specs=row,
        out_shape=jax.ShapeDtypeStruct((n, D_MODEL), F32),
        scratch_shapes=[pltpu.VMEM((tm, D_FF), BF16)],
        compiler_params=pltpu.CompilerParams(
            dimension_semantics=("arbitrary",), vmem_limit_bytes=VMEM_LIMIT),
        name="ffn",
    )(x, g, wg, wu, wd, fg)


def _windowed_gqa(q0, qa_ref, ka_ref, va_ref, sink_ref, seq):
    tq = qa_ref.shape[0]
    mlo, mhi = _half_masks(BF16)
    lane = lax.broadcasted_iota(jnp.int32, (A_BLOCK, LANES), 1)
    a_idx = lax.broadcasted_iota(jnp.int32, (A_BLOCK, A_KEYS), 0)
    b_idx = lax.broadcasted_iota(jnp.int32, (A_BLOCK, A_KEYS), 1)
    row_blocks = []
    for r in range(tq // A_BLOCK):
        qs = q0 + r * A_BLOCK
        ws = pl.multiple_of(jnp.clip(qs - A_BLOCK, 0, seq - A_KEYS), A_BLOCK)
        dist = jnp.abs((qs - ws) + a_idx - b_idx)
        valid = dist <= WINDOW
        distf = dist.astype(F32)
        slabs = []
        for kv in range(A_KV_HEADS):
            kwin = ka_ref[pl.ds(ws, A_KEYS), kv * LANES:(kv + 1) * LANES]
            vwin = va_ref[pl.ds(ws, A_KEYS), kv * LANES:(kv + 1) * LANES]
            parts = []
            for g in range(A_GROUP):
                h = kv * A_GROUP + g
                slab = qa_ref[r * A_BLOCK:(r + 1) * A_BLOCK, (h // 2) * LANES:(h // 2 + 1) * LANES]
                parts.append(slab * (mlo if h % 2 == 0 else mhi))
            s = _dot_nt(jnp.concatenate(parts, axis=0), kwin)
            es, dens = [], []
            for g in range(A_GROUP):
                h = kv * A_GROUP + g
                sg = s[g * A_BLOCK:(g + 1) * A_BLOCK]
                sg = jnp.where(valid, sg - SLOPES_A[h] * distf, NEG_INF)
                sk = sink_ref[h]
                m = jnp.maximum(jnp.max(sg, axis=-1, keepdims=True), sk)
                e = jnp.exp(sg - m)
                dens.append(jnp.sum(e, axis=-1, keepdims=True) + jnp.exp(sk - m))
                es.append(e.astype(BF16))
            pv = _dot(jnp.concatenate(es, axis=0), vwin)
            outs = [pv[g * A_BLOCK:(g + 1) * A_BLOCK] / dens[g] for g in range(A_GROUP)]
            slabs.append(jnp.where(lane < HEAD_DIM, outs[0], outs[1]))
            slabs.append(jnp.where(lane < HEAD_DIM, outs[2], outs[3]))
        row_blocks.append(jnp.concatenate(slabs, axis=1))
    return jnp.concatenate(row_blocks, axis=0)


def _attn_kernel(x_ref, qa_ref, qb_ref, qaug_ref, ka_ref, va_ref, kb_ref, vb_ref, kaug_ref,
                 wout_ref, sink_ref, lq1_ref, lk1_ref, lq2_ref, lk2_ref, subln_ref,
                 xo_ref, s0_scr, s1_scr, p0_scr, p1_scr, ksgn_scr, fd_scr, *, lam_init):
    tq = x_ref.shape[0]
    seq = kb_ref.shape[1]
    q0 = pl.program_id(1) * tq

    lam = (jnp.exp(jnp.sum(lq1_ref[...] * lk1_ref[...], axis=-1, keepdims=True))
           - jnp.exp(jnp.sum(lq2_ref[...] * lk2_ref[...], axis=-1, keepdims=True)) + lam_init)

    kpos = lax.broadcasted_iota(jnp.int32, (seq, LANES), 0)
    kaug = kaug_ref[...]
    ksgn_scr[...] = jnp.where(kpos < q0 + tq, kaug, -kaug).astype(BF16)
    r_idx = lax.broadcasted_iota(jnp.int32, (2 * tq, tq), 0)
    c_idx = lax.broadcasted_iota(jnp.int32, (2 * tq, tq), 1)
    a_idx = jnp.where(r_idx >= tq, r_idx - tq, r_idx)
    fd_scr[...] = (2 * jnp.maximum(c_idx - a_idx, 0)).astype(F32)

    mlo, mhi = _half_masks(BF16)
    ones = jnp.ones((seq, LANES), BF16)
    diag = pl.ds(pl.multiple_of(q0, tq), tq)
    s_bufs = (s0_scr, s1_scr)
    p_bufs = (p0_scr, p1_scr)

    def scores(h):
        s_scr = s_bufs[h % 2]
        q = qb_ref[h]
        aq = qaug_ref[h]
        lhs = jnp.concatenate(
            [jnp.concatenate([q * mlo, q * mhi], axis=0), jnp.concatenate([aq, aq], axis=0)], axis=1)
        rhs = jnp.concatenate([kb_ref[h], ksgn_scr[...]], axis=1)
        s_scr[...] = _dot_nt(lhs, rhs)
        s_scr[:, diag] = s_scr[:, diag] - SLOPES_B[h] * fd_scr[...]

    def softmax(h):
        s_scr, p_scr = s_bufs[h % 2], p_bufs[h % 2]
        for r in range(2 * tq // SOFTMAX_ROWS):
            rows = slice(r * SOFTMAX_ROWS, (r + 1) * SOFTMAX_ROWS)
            s = s_scr[rows, :]
            m = jnp.max(s, axis=-1, keepdims=True)
            p_scr[rows, :] = jnp.exp(s - m).astype(BF16)

    def values(h):
        ol = _dot(p_bufs[h % 2][...], jnp.concatenate([vb_ref[h], ones], axis=1))
        o1 = ol[:tq, :LANES] / ol[:tq, LANES:]
        o2 = ol[tq:, :LANES] / ol[tq:, LANES:]
        o = o1 - lam * o2
        return (_rms(o, subln_ref[...]) * (1.0 - lam_init)).astype(BF16)

    ob = [None] * B_HEADS
    for step in range(B_HEADS + 2):
        if step < B_HEADS:
            scores(step)
        if 1 <= step <= B_HEADS:
            softmax(step - 1)
        if step >= 2:
            ob[step - 2] = values(step - 2)

    oa = _windowed_gqa(q0, qa_ref, ka_ref, va_ref, sink_ref, seq).astype(BF16)

    mix = jnp.concatenate([oa] + ob, axis=1)
    xo_ref[...] = x_ref[...] + _dot(mix, wout_ref[...])


def _attn_call(x, qa, ka, va, qb, kb, vb, qaug, kaug, wout, sink,
               lq1, lk1, lq2, lk2, subln, batch, seq, lam_init):
    n = x.shape[0]
    tq = TQ_ATTN
    nq = seq // tq
    tile = lambda cols: pl.BlockSpec((tq, cols), lambda b, i: (b * nq + i, 0))
    per_batch = lambda cols: pl.BlockSpec((seq, cols), lambda b, i: (b, 0), pipeline_mode=pl.Buffered(1))
    head_tile = pl.BlockSpec((B_HEADS, tq, LANES), lambda b, i: (0, b * nq + i, 0))
    head_batch = pl.BlockSpec((B_HEADS, seq, LANES), lambda b, i: (0, b, 0), pipeline_mode=pl.Buffered(1))
    smem = pl.BlockSpec(memory_space=pltpu.SMEM)
    small = lambda cols: _resident((1, cols))
    return pl.pallas_call(
        functools.partial(_attn_kernel, lam_init=lam_init),
        grid=(batch, nq),
        in_specs=[tile(D_MODEL), tile(A_Q_COLS), head_tile,
                  pl.BlockSpec((B_HEADS, tq, LANES), lambda b, i: (0, i, 0)),
                  per_batch(A_KV_DUP_COLS), per_batch(A_KV_DUP_COLS), head_batch, head_batch,
                  _resident((seq, LANES)), _resident((MIX_WIDTH, D_MODEL)), smem,
                  small(HEAD_DIM), small(HEAD_DIM), small(HEAD_DIM), small(HEAD_DIM), small(LANES)],
        out_specs=tile(D_MODEL),
        out_shape=jax.ShapeDtypeStruct((n, D_MODEL), F32),
        scratch_shapes=[pltpu.VMEM((2 * tq, seq), F32), pltpu.VMEM((2 * tq, seq), F32),
                        pltpu.VMEM((2 * tq, seq), BF16), pltpu.VMEM((2 * tq, seq), BF16),
                        pltpu.VMEM((seq, LANES), BF16), pltpu.VMEM((2 * tq, tq), F32)],
        compiler_params=pltpu.CompilerParams(
            dimension_semantics=("arbitrary", "arbitrary"), vmem_limit_bytes=VMEM_LIMIT),
        name="attn",
    )(x, qa, qb, qaug, ka, va, kb, vb, kaug, wout, sink, lq1, lk1, lq2, lk2, subln)


def _position_tables(seq):
    pos = jnp.arange(seq, dtype=jnp.int32)
    lo = (pos % LANES).astype(F32)
    hi = (LANES * (pos // LANES)).astype(F32)
    one = jnp.ones((seq,), F32)
    pad = jnp.zeros((seq, LANES - 4), F32)
    kaug = jnp.concatenate([jnp.stack([lo, hi, one, one], axis=1), pad], axis=1)
    slopes = jnp.asarray(SLOPES_B, F32)[:, None, None]
    qcols = jnp.stack([one, one, -lo, -hi], axis=1)[None]
    qaug = jnp.concatenate([slopes * qcols, jnp.zeros((B_HEADS, seq, LANES - 4), F32)], axis=2)
    return qaug.astype(BF16), kaug


def _prep_w_in(w):
    scale = HEAD_DIM ** -0.5
    c = 0
    qa = w[:, c:c + A_Q_COLS] * scale; c += A_Q_COLS
    ka = w[:, c:c + A_KV_HEADS * HEAD_DIM]; c += A_KV_HEADS * HEAD_DIM
    va = w[:, c:c + A_KV_HEADS * HEAD_DIM]; c += A_KV_HEADS * HEAD_DIM
    qb = w[:, c:c + B_COLS] * scale; c += B_COLS
    kb = w[:, c:c + B_COLS]; c += B_COLS
    vb = w[:, c:c + B_COLS]
    dup = lambda t: jnp.concatenate(
        [t[:, j * HEAD_DIM:(j + 1) * HEAD_DIM] for j in range(A_KV_HEADS) for _ in range(2)], axis=1)
    return jnp.concatenate([qa, dup(ka), dup(va), qb, kb, vb], axis=1).astype(BF16)


def kernel(x, ffn1_norm, ffn1_w_gate, ffn1_w_up, ffn1_w_down, mix_norm, w_in, sink,
           lam_q1, lam_k1, lam_q2, lam_k2, diff_subln, w_out,
           ffn2_norm, ffn2_w_gate, ffn2_w_up, ffn2_w_down, final_norm):
    batch, seq, d = x.shape
    depth = w_in.shape[0]
    assert d == D_MODEL and seq % TQ_ATTN == 0 and (batch * seq) % TM_FFN == 0
    xf = x.reshape(batch * seq, d)
    qaug, kaug = _position_tables(seq)
    row = lambda v: v.reshape(1, -1).astype(F32)
    for l in range(depth):
        lam_init = 0.8 - 0.6 * math.exp(-0.3 * l)
        xf, qa, ka, va, qb, kb, vb = _ffn_inproj_call(
            xf, row(ffn1_norm[l]), ffn1_w_gate[l].astype(BF16), ffn1_w_up[l].astype(BF16),
            ffn1_w_down[l].astype(BF16), row(mix_norm[l]), _prep_w_in(w_in[l]))
        xf = _attn_call(xf, qa, ka, va, qb, kb, vb, qaug, kaug, w_out[l].astype(BF16),
                        sink[l].astype(F32), row(lam_q1[l]), row(lam_k1[l]),
                        row(lam_q2[l]), row(lam_k2[l]), row(diff_subln[l]), batch, seq, lam_init)
        xf = _ffn_call(xf, row(ffn2_norm[l]), ffn2_w_gate[l].astype(BF16), ffn2_w_up[l].astype(BF16),
                       ffn2_w_down[l].astype(BF16), row(final_norm), final_norm=(l == depth - 1))
    return xf.reshape(batch, seq, d)
```

```python
import functools
import math

import jax
import jax.numpy as jnp
from jax import lax
from jax.experimental import pallas as pl
from jax.experimental.pallas import tpu as pltpu

F32 = jnp.float32
BF16 = jnp.bfloat16

D_MODEL = 1024
D_FF = 2816
HEAD_DIM = 64
WINDOW = 128
A_Q_HEADS = 8
A_KV_HEADS = 2
A_GROUP = A_Q_HEADS // A_KV_HEADS
B_HEADS = 4
RMS_EPS = 1e-6
NEG_INF = -1e30

LANES = 128
MXU_DIM = 256
A_Q_COLS = A_Q_HEADS * HEAD_DIM
B_COLS = B_HEADS * LANES
A_KV_DUP_COLS = 2 * A_KV_HEADS * HEAD_DIM
IN_COLS_DUP = A_Q_COLS + 2 * A_KV_DUP_COLS + 3 * B_COLS
MIX_WIDTH = A_Q_COLS + B_COLS

FF_CHUNK = MXU_DIM
TM_FFN = 512
TQ_ATTN = 256
A_BLOCK = 128
A_KEYS = 3 * A_BLOCK
SOFTMAX_ROWS = 32
VMEM_LIMIT = 62 * 1024 * 1024

SLOPES_A = tuple(2.0 ** (-8.0 * i / A_Q_HEADS) for i in range(1, A_Q_HEADS + 1))
SLOPES_B = tuple(2.0 ** (-8.0 * i / B_HEADS) for i in range(1, B_HEADS + 1))


def _rms(x, g):
    ms = jnp.mean(x * x, axis=-1, keepdims=True)
    return x * lax.rsqrt(ms + RMS_EPS) * g


def _dot(a, b):
    return jnp.dot(a, b, preferred_element_type=F32)


def _dot_nt(a, b):
    return lax.dot_general(a, b, (((1,), (1,)), ((), ())), preferred_element_type=F32)


def _half_masks(dtype):
    lane = lax.broadcasted_iota(jnp.int32, (1, LANES), 1)
    lo = (lane < HEAD_DIM).astype(F32).astype(dtype)
    hi = (lane >= HEAD_DIM).astype(F32).astype(dtype)
    return lo, hi


def _swiglu_residual(x, g_ref, wg_ref, wu_ref, wd_ref, h_scr):
    xn = _rms(x, g_ref[...]).astype(BF16)
    for c in range(D_FF // FF_CHUNK):
        sl = slice(c * FF_CHUNK, (c + 1) * FF_CHUNK)
        gate = _dot(xn, wg_ref[:, sl])
        up = _dot(xn, wu_ref[:, sl])
        h_scr[:, sl] = (gate * jax.nn.sigmoid(gate) * up).astype(BF16)
    return x + 0.5 * _dot(h_scr[...], wd_ref[...])


def _ffn_inproj_kernel(x_ref, g_ref, wg_ref, wu_ref, wd_ref, mg_ref, win_ref,
                       xo_ref, qa_ref, ka_ref, va_ref, qb_ref, kb_ref, vb_ref, h_scr):
    x1 = _swiglu_residual(x_ref[...], g_ref, wg_ref, wu_ref, wd_ref, h_scr)
    xo_ref[...] = x1
    hn = _rms(x1, mg_ref[...]).astype(BF16)
    proj = _dot(hn, win_ref[...]).astype(BF16)
    c0 = A_Q_COLS
    c1 = c0 + A_KV_DUP_COLS
    c2 = c1 + A_KV_DUP_COLS
    qa_ref[...] = proj[:, :c0]
    ka_ref[...] = proj[:, c0:c1]
    va_ref[...] = proj[:, c1:c2]
    for h in range(B_HEADS):
        qb_ref[h] = proj[:, c2 + h * LANES: c2 + (h + 1) * LANES]
        kb_ref[h] = proj[:, c2 + B_COLS + h * LANES: c2 + B_COLS + (h + 1) * LANES]
        vb_ref[h] = proj[:, c2 + 2 * B_COLS + h * LANES: c2 + 2 * B_COLS + (h + 1) * LANES]


def _ffn_kernel(x_ref, g_ref, wg_ref, wu_ref, wd_ref, fg_ref, xo_ref, h_scr, *, final_norm):
    x1 = _swiglu_residual(x_ref[...], g_ref, wg_ref, wu_ref, wd_ref, h_scr)
    if final_norm:
        x1 = _rms(x1, fg_ref[...])
    xo_ref[...] = x1


def _resident(shape):
    nd = len(shape)
    return pl.BlockSpec(shape, lambda *_: (0,) * nd, pipeline_mode=pl.Buffered(1))


def _ffn_weight_specs():
    return [_resident((1, D_MODEL)), _resident((D_MODEL, D_FF)), _resident((D_MODEL, D_FF)),
            _resident((D_FF, D_MODEL))]


def _ffn_inproj_call(x, g, wg, wu, wd, mg, win):
    n = x.shape[0]
    tm = TM_FFN
    row = lambda cols: pl.BlockSpec((tm, cols), lambda i: (i, 0))
    headrow = pl.BlockSpec((B_HEADS, tm, LANES), lambda i: (0, i, 0))
    out_shape = (
        jax.ShapeDtypeStruct((n, D_MODEL), F32),
        jax.ShapeDtypeStruct((n, A_Q_COLS), BF16),
        jax.ShapeDtypeStruct((n, A_KV_DUP_COLS), BF16),
        jax.ShapeDtypeStruct((n, A_KV_DUP_COLS), BF16),
        jax.ShapeDtypeStruct((B_HEADS, n, LANES), BF16),
        jax.ShapeDtypeStruct((B_HEADS, n, LANES), BF16),
        jax.ShapeDtypeStruct((B_HEADS, n, LANES), BF16),
    )
    return pl.pallas_call(
        _ffn_inproj_kernel,
        grid=(n // tm,),
        in_specs=[row(D_MODEL)] + _ffn_weight_specs()
                 + [_resident((1, D_MODEL)), _resident((D_MODEL, IN_COLS_DUP))],
        out_specs=(row(D_MODEL), row(A_Q_COLS), row(A_KV_DUP_COLS), row(A_KV_DUP_COLS),
                   headrow, headrow, headrow),
        out_shape=out_shape,
        scratch_shapes=[pltpu.VMEM((tm, D_FF), BF16)],
        compiler_params=pltpu.CompilerParams(
            dimension_semantics=("arbitrary",), vmem_limit_bytes=VMEM_LIMIT),
        name="ffn_inproj",
    )(x, g, wg, wu, wd, mg, win)


def _ffn_call(x, g, wg, wu, wd, fg, final_norm):
    n = x.shape[0]
    tm = TM_FFN
    row = pl.BlockSpec((tm, D_MODEL), lambda i: (i, 0))
    return pl.pallas_call(
        functools.partial(_ffn_kernel, final_norm=final_norm),
        grid=(n // tm,),
        in_specs=[row] + _ffn_weight_specs() + [_resident((1, D_MODEL))],
        out_specs=row,
        out_shape=jax.ShapeDtypeStruct((n, D_MODEL), F32),
        scratch_shapes=[pltpu.VMEM((tm, D_FF), BF16)],
        compiler_params=pltpu.CompilerParams(
            dimension_semantics=("arbitrary",), vmem_limit_bytes=VMEM_LIMIT),
        name="ffn",
    )(x, g, wg, wu, wd, fg)


def _windowed_gqa(q0, qa_ref, ka_ref, va_ref, sink_ref, seq):
    tq = qa_ref.shape[0]
    mlo, mhi = _half_masks(BF16)
    lane = lax.broadcasted_iota(jnp.int32, (A_BLOCK, LANES), 1)
    a_idx = lax.broadcasted_iota(jnp.int32, (A_BLOCK, A_KEYS), 0)
    b_idx = lax.broadcasted_iota(jnp.int32, (A_BLOCK, A_KEYS), 1)
    row_blocks = []
    for r in range(tq // A_BLOCK):
        qs = q0 + r * A_BLOCK
        ws = pl.multiple_of(jnp.clip(qs - A_BLOCK, 0, seq - A_KEYS), A_BLOCK)
        dist = jnp.abs((qs - ws) + a_idx - b_idx)
        valid = dist <= WINDOW
        distf = dist.astype(F32)
        slabs = []
        for kv in range(A_KV_HEADS):
            kwin = ka_ref[pl.ds(ws, A_KEYS), kv * LANES:(kv + 1) * LANES]
            vwin = va_ref[pl.ds(ws, A_KEYS), kv * LANES:(kv + 1) * LANES]
            parts = []
            for g in range(A_GROUP):
                h = kv * A_GROUP + g
                slab = qa_ref[r * A_BLOCK:(r + 1) * A_BLOCK, (h // 2) * LANES:(h // 2 + 1) * LANES]
                parts.append(slab * (mlo if h % 2 == 0 else mhi))
            s = _dot_nt(jnp.concatenate(parts, axis=0), kwin)
            es, dens = [], []
            for g in range(A_GROUP):
                h = kv * A_GROUP + g
                sg = s[g * A_BLOCK:(g + 1) * A_BLOCK]
                sg = jnp.where(valid, sg - SLOPES_A[h] * distf, NEG_INF)
                sk = sink_ref[h]
                m = jnp.maximum(jnp.max(sg, axis=-1, keepdims=True), sk)
                e = jnp.exp(sg - m)
                dens.append(jnp.sum(e, axis=-1, keepdims=True) + jnp.exp(sk - m))
                es.append(e.astype(BF16))
            pv = _dot(jnp.concatenate(es, axis=0), vwin)
            outs = [pv[g * A_BLOCK:(g + 1) * A_BLOCK] / dens[g] for g in range(A_GROUP)]
            slabs.append(jnp.where(lane < HEAD_DIM, outs[0], outs[1]))
            slabs.append(jnp.where(lane < HEAD_DIM, outs[2], outs[3]))
        row_blocks.append(jnp.concatenate(slabs, axis=1))
    return jnp.concatenate(row_blocks, axis=0)


def _attn_kernel(x_ref, qa_ref, qb_ref, qaug_ref, ka_ref, va_ref, kb_ref, vb01_ref, vb23_ref, kaug_ref,
                 wout_ref, sink_ref, lq1_ref, lk1_ref, lq2_ref, lk2_ref, subln_ref,
                 xo_ref, s0_scr, s1_scr, p0_scr, p1_scr, ksgn_scr, fd_scr, ob_scr, *, lam_init, nq, n_tiles):
    tq = x_ref.shape[0]
    seq = kb_ref.shape[1]
    t = pl.program_id(0)
    q0 = lax.rem(jnp.minimum(t, n_tiles - 1), nq) * tq
    q0_prev = lax.rem(jnp.maximum(t - 1, 0), nq) * tq
    par = lax.rem(t, 2)

    @pl.when(t == 0)
    def _():
        s1_scr[...] = jnp.zeros_like(s1_scr)
        p0_scr[...] = jnp.zeros_like(p0_scr)
        ob_scr[1] = jnp.zeros((2, tq, LANES), BF16)

    lam = (jnp.exp(jnp.sum(lq1_ref[...] * lk1_ref[...], axis=-1, keepdims=True))
           - jnp.exp(jnp.sum(lq2_ref[...] * lk2_ref[...], axis=-1, keepdims=True)) + lam_init)

    kpos = lax.broadcasted_iota(jnp.int32, (seq, LANES), 0)
    kaug = kaug_ref[...]
    ksgn_scr[...] = jnp.where(kpos < q0 + tq, kaug, -kaug).astype(BF16)
    r_idx = lax.broadcasted_iota(jnp.int32, (2 * tq, tq), 0)
    c_idx = lax.broadcasted_iota(jnp.int32, (2 * tq, tq), 1)
    a_idx = jnp.where(r_idx >= tq, r_idx - tq, r_idx)
    fd_scr[...] = (2 * jnp.maximum(c_idx - a_idx, 0)).astype(F32)

    mlo, mhi = _half_masks(BF16)
    ones = jnp.ones((seq, LANES), BF16)
    diag = pl.ds(pl.multiple_of(q0, tq), tq)

    def scores(h, s_scr):
        q = qb_ref[h]
        aq = qaug_ref[h]
        lhs = jnp.concatenate(
            [jnp.concatenate([q * mlo, q * mhi], axis=0), jnp.concatenate([aq, aq], axis=0)], axis=1)
        rhs = jnp.concatenate([kb_ref[h], ksgn_scr[...]], axis=1)
        s_scr[...] = _dot_nt(lhs, rhs)
        s_scr[:, diag] = s_scr[:, diag] - SLOPES_B[h] * fd_scr[...]

    def softmax(s_scr, p_scr):
        for r in range(2 * tq // SOFTMAX_ROWS):
            rows = slice(r * SOFTMAX_ROWS, (r + 1) * SOFTMAX_ROWS)
            s = s_scr[rows, :]
            m = jnp.max(s, axis=-1, keepdims=True)
            p_scr[rows, :] = jnp.exp(s - m).astype(BF16)

    def values(h, p_scr):
        v = vb01_ref[h] if h < 2 else vb23_ref[h - 2]
        ol = _dot(p_scr[...], jnp.concatenate([v, ones], axis=1))
        o1 = ol[:tq, :LANES] / ol[:tq, LANES:]
        o2 = ol[tq:, :LANES] / ol[tq:, LANES:]
        o = o1 - lam * o2
        return (_rms(o, subln_ref[...]) * (1.0 - lam_init)).astype(BF16)

    scores(0, s0_scr)
    softmax(s1_scr, p1_scr)
    ob2 = values(2, p0_scr)

    scores(1, s1_scr)
    softmax(s0_scr, p0_scr)
    ob3 = values(3, p1_scr)

    scores(2, s0_scr)
    softmax(s1_scr, p1_scr)
    ob_scr[par, 0] = values(0, p0_scr)

    scores(3, s1_scr)
    softmax(s0_scr, p0_scr)
    ob_scr[par, 1] = values(1, p1_scr)

    oa = _windowed_gqa(q0_prev, qa_ref, ka_ref, va_ref, sink_ref, seq).astype(BF16)
    mix = jnp.concatenate([oa, ob_scr[1 - par, 0], ob_scr[1 - par, 1], ob2, ob3], axis=1)
    xo_ref[...] = x_ref[...] + _dot(mix, wout_ref[...])


def _attn_call(x, qa, ka, va, qb, kb, vb, qaug, kaug, wout, sink,
               lq1, lk1, lq2, lk2, subln, batch, seq, lam_init):
    n = x.shape[0]
    tq = TQ_ATTN
    nq = seq // tq
    n_tiles = batch * nq
    cur = lambda t: jnp.minimum(t, n_tiles - 1)
    prev = lambda t: jnp.maximum(t - 1, 0)
    once = pl.Buffered(1)
    tile_prev = lambda cols: pl.BlockSpec((tq, cols), lambda t: (prev(t), 0))
    seq_prev = lambda cols: pl.BlockSpec((seq, cols), lambda t: (prev(t) // nq, 0), pipeline_mode=once)
    smem = pl.BlockSpec(memory_space=pltpu.SMEM)
    small = lambda cols: _resident((1, cols))
    return pl.pallas_call(
        functools.partial(_attn_kernel, lam_init=lam_init, nq=nq, n_tiles=n_tiles),
        grid=(n_tiles + 1,),
        in_specs=[tile_prev(D_MODEL), tile_prev(A_Q_COLS),
                  pl.BlockSpec((B_HEADS, tq, LANES), lambda t: (0, cur(t), 0)),
                  pl.BlockSpec((B_HEADS, tq, LANES), lambda t: (0, cur(t) % nq, 0)),
                  seq_prev(A_KV_DUP_COLS), seq_prev(A_KV_DUP_COLS),
                  pl.BlockSpec((B_HEADS, seq, LANES), lambda t: (0, cur(t) // nq, 0), pipeline_mode=once),
                  pl.BlockSpec((2, seq, LANES), lambda t: (0, cur(t) // nq, 0), pipeline_mode=once),
                  pl.BlockSpec((2, seq, LANES), lambda t: (1, prev(t) // nq, 0), pipeline_mode=once),
                  _resident((seq, LANES)), _resident((MIX_WIDTH, D_MODEL)), smem,
                  small(HEAD_DIM), small(HEAD_DIM), small(HEAD_DIM), small(HEAD_DIM), small(LANES)],
        out_specs=tile_prev(D_MODEL),
        out_shape=jax.ShapeDtypeStruct((n, D_MODEL), F32),
        scratch_shapes=[pltpu.VMEM((2 * tq, seq), F32), pltpu.VMEM((2 * tq, seq), F32),
                        pltpu.VMEM((2 * tq, seq), BF16), pltpu.VMEM((2 * tq, seq), BF16),
                        pltpu.VMEM((seq, LANES), BF16), pltpu.VMEM((2 * tq, tq), F32),
                        pltpu.VMEM((2, 2, tq, LANES), BF16)],
        compiler_params=pltpu.CompilerParams(
            dimension_semantics=("arbitrary",), vmem_limit_bytes=VMEM_LIMIT),
        name="attn",
    )(x, qa, qb, qaug, ka, va, kb, vb, vb, kaug, wout, sink, lq1, lk1, lq2, lk2, subln)


def _position_tables(seq):
    pos = jnp.arange(seq, dtype=jnp.int32)
    lo = (pos % LANES).astype(F32)
    hi = (LANES * (pos // LANES)).astype(F32)
    one = jnp.ones((seq,), F32)
    pad = jnp.zeros((seq, LANES - 4), F32)
    kaug = jnp.concatenate([jnp.stack([lo, hi, one, one], axis=1), pad], axis=1)
    slopes = jnp.asarray(SLOPES_B, F32)[:, None, None]
    qcols = jnp.stack([one, one, -lo, -hi], axis=1)[None]
    qaug = jnp.concatenate([slopes * qcols, jnp.zeros((B_HEADS, seq, LANES - 4), F32)], axis=2)
    return qaug.astype(BF16), kaug


def _prep_w_in(w):
    scale = HEAD_DIM ** -0.5
    c = 0
    qa = w[:, c:c + A_Q_COLS] * scale; c += A_Q_COLS
    ka = w[:, c:c + A_KV_HEADS * HEAD_DIM]; c += A_KV_HEADS * HEAD_DIM
    va = w[:, c:c + A_KV_HEADS * HEAD_DIM]; c += A_KV_HEADS * HEAD_DIM
    qb = w[:, c:c + B_COLS] * scale; c += B_COLS
    kb = w[:, c:c + B_COLS]; c += B_COLS
    vb = w[:, c:c + B_COLS]
    dup = lambda t: jnp.concatenate(
        [t[:, j * HEAD_DIM:(j + 1) * HEAD_DIM] for j in range(A_KV_HEADS) for _ in range(2)], axis=1)
    return jnp.concatenate([qa, dup(ka), dup(va), qb, kb, vb], axis=1).astype(BF16)


def kernel(x, ffn1_norm, ffn1_w_gate, ffn1_w_up, ffn1_w_down, mix_norm, w_in, sink,
           lam_q1, lam_k1, lam_q2, lam_k2, diff_subln, w_out,
           ffn2_norm, ffn2_w_gate, ffn2_w_up, ffn2_w_down, final_norm):
    batch, seq, d = x.shape
    depth = w_in.shape[0]
    assert d == D_MODEL and seq % TQ_ATTN == 0 and (batch * seq) % TM_FFN == 0
    xf = x.reshape(batch * seq, d)
    qaug, kaug = _position_tables(seq)
    row = lambda v: v.reshape(1, -1).astype(F32)
    for l in range(depth):
        lam_init = 0.8 - 0.6 * math.exp(-0.3 * l)
        xf, qa, ka, va, qb, kb, vb = _ffn_inproj_call(
            xf, row(ffn1_norm[l]), ffn1_w_gate[l].astype(BF16), ffn1_w_up[l].astype(BF16),
            ffn1_w_down[l].astype(BF16), row(mix_norm[l]), _prep_w_in(w_in[l]))
        xf = _attn_call(xf, qa, ka, va, qb, kb, vb, qaug, kaug, w_out[l].astype(BF16),
                        sink[l].astype(F32), row(lam_q1[l]), row(lam_k1[l]),
                        row(lam_q2[l]), row(lam_k2[l]), row(diff_subln[l]), batch, seq, lam_init)
        xf = _ffn_call(xf, row(ffn2_norm[l]), ffn2_w_gate[l].astype(BF16), ffn2_w_up[l].astype(BF16),
                       ffn2_w_down[l].astype(BF16), row(final_norm), final_norm=(l == depth - 1))
    return xf.reshape(batch, seq, d)
```

```python
import functools
import math

import jax
import jax.numpy as jnp
from jax import lax
from jax.experimental import pallas as pl
from jax.experimental.pallas import tpu as pltpu

F32 = jnp.float32
BF16 = jnp.bfloat16

D_MODEL = 1024
D_FF = 2816
HEAD_DIM = 64
WINDOW = 128
A_Q_HEADS = 8
A_KV_HEADS = 2
A_GROUP = A_Q_HEADS // A_KV_HEADS
B_HEADS = 4
RMS_EPS = 1e-6
NEG_INF = -1e30

LANES = 128
MXU_DIM = 256
BF16_ROWS = 16
A_Q_COLS = A_Q_HEADS * HEAD_DIM
B_COLS = B_HEADS * LANES
A_KV_DUP_COLS = 2 * A_KV_HEADS * HEAD_DIM
IN_COLS_DUP = A_Q_COLS + 2 * A_KV_DUP_COLS + 2 * B_COLS
MIX_WIDTH = A_Q_COLS + B_COLS

FF_CHUNK = MXU_DIM
TM_FFN = 512
TQ_ATTN = 256
A_BLOCK = 128
A_KEYS = 3 * A_BLOCK
EXP_ROWS = 64
DIAG_COL = 4
DIAG_MASK = -32768.0
VMEM_LIMIT = 56 * 1024 * 1024

SLOPES_A = tuple(2.0 ** (-8.0 * i / A_Q_HEADS) for i in range(1, A_Q_HEADS + 1))
SLOPES_B = tuple(2.0 ** (-8.0 * i / B_HEADS) for i in range(1, B_HEADS + 1))


def _rms(x, g):
    ms = jnp.mean(x * x, axis=-1, keepdims=True)
    return x * lax.rsqrt(ms + RMS_EPS) * g


def _dot(a, b):
    return jnp.dot(a, b, preferred_element_type=F32)


def _dot_nt(a, b):
    return lax.dot_general(a, b, (((1,), (1,)), ((), ())), preferred_element_type=F32)


def _half_masks(dtype):
    lane = lax.broadcasted_iota(jnp.int32, (1, LANES), 1)
    lo = (lane < HEAD_DIM).astype(F32).astype(dtype)
    hi = (lane >= HEAD_DIM).astype(F32).astype(dtype)
    return lo, hi


def _swiglu_residual(x, g_ref, wg_ref, wu_ref, wd_ref, h_scr):
    xn = _rms(x, g_ref[...]).astype(BF16)
    for c in range(D_FF // FF_CHUNK):
        sl = slice(c * FF_CHUNK, (c + 1) * FF_CHUNK)
        gate = _dot(xn, wg_ref[:, sl])
        up = _dot(xn, wu_ref[:, sl])
        h_scr[:, sl] = (gate * jax.nn.sigmoid(gate) * up).astype(BF16)
    return x + 0.5 * _dot(h_scr[...], wd_ref[...])


def _ffn_inproj_kernel(x_ref, g_ref, wg_ref, wu_ref, wd_ref, mg_ref, win_ref, wvt_ref,
                       xo_ref, qa_ref, ka_ref, va_ref, qb_ref, kb_ref, vbt_ref, h_scr):
    x1 = _swiglu_residual(x_ref[...], g_ref, wg_ref, wu_ref, wd_ref, h_scr)
    xo_ref[...] = x1
    hn = _rms(x1, mg_ref[...]).astype(BF16)
    proj = _dot(hn, win_ref[...]).astype(BF16)
    c0 = A_Q_COLS
    c1 = c0 + A_KV_DUP_COLS
    c2 = c1 + A_KV_DUP_COLS
    qa_ref[...] = proj[:, :c0]
    ka_ref[...] = proj[:, c0:c1]
    va_ref[...] = proj[:, c1:c2]
    vt = _dot_nt(wvt_ref[...], hn).astype(BF16)
    for h in range(B_HEADS):
        qb_ref[h] = proj[:, c2 + h * LANES: c2 + (h + 1) * LANES]
        kb_ref[h] = proj[:, c2 + B_COLS + h * LANES: c2 + B_COLS + (h + 1) * LANES]
        vbt_ref[h] = vt[h * LANES:(h + 1) * LANES, :]


def _ffn_kernel(x_ref, g_ref, wg_ref, wu_ref, wd_ref, fg_ref, xo_ref, h_scr, *, final_norm):
    x1 = _swiglu_residual(x_ref[...], g_ref, wg_ref, wu_ref, wd_ref, h_scr)
    if final_norm:
        x1 = _rms(x1, fg_ref[...])
    xo_ref[...] = x1


def _resident(shape):
    nd = len(shape)
    return pl.BlockSpec(shape, lambda *_: (0,) * nd, pipeline_mode=pl.Buffered(1))


def _ffn_weight_specs():
    return [_resident((1, D_MODEL)), _resident((D_MODEL, D_FF)), _resident((D_MODEL, D_FF)),
            _resident((D_FF, D_MODEL))]


def _ffn_inproj_call(x, g, wg, wu, wd, mg, win, wvt):
    n = x.shape[0]
    tm = TM_FFN
    row = lambda cols: pl.BlockSpec((tm, cols), lambda i: (i, 0))
    headrow = pl.BlockSpec((B_HEADS, tm, LANES), lambda i: (0, i, 0))
    headcol = pl.BlockSpec((B_HEADS, LANES, tm), lambda i: (0, 0, i))
    out_shape = (
        jax.ShapeDtypeStruct((n, D_MODEL), F32),
        jax.ShapeDtypeStruct((n, A_Q_COLS), BF16),
        jax.ShapeDtypeStruct((n, A_KV_DUP_COLS), BF16),
        jax.ShapeDtypeStruct((n, A_KV_DUP_COLS), BF16),
        jax.ShapeDtypeStruct((B_HEADS, n, LANES), BF16),
        jax.ShapeDtypeStruct((B_HEADS, n, LANES), BF16),
        jax.ShapeDtypeStruct((B_HEADS, LANES, n), BF16),
    )
    return pl.pallas_call(
        _ffn_inproj_kernel,
        grid=(n // tm,),
        in_specs=[row(D_MODEL)] + _ffn_weight_specs()
                 + [_resident((1, D_MODEL)), _resident((D_MODEL, IN_COLS_DUP)), _resident((B_COLS, D_MODEL))],
        out_specs=(row(D_MODEL), row(A_Q_COLS), row(A_KV_DUP_COLS), row(A_KV_DUP_COLS),
                   headrow, headrow, headcol),
        out_shape=out_shape,
        scratch_shapes=[pltpu.VMEM((tm, D_FF), BF16)],
        compiler_params=pltpu.CompilerParams(
            dimension_semantics=("arbitrary",), vmem_limit_bytes=VMEM_LIMIT),
        name="ffn_inproj",
    )(x, g, wg, wu, wd, mg, win, wvt)


def _ffn_call(x, g, wg, wu, wd, fg, final_norm):
    n = x.shape[0]
    tm = TM_FFN
    row = pl.BlockSpec((tm, D_MODEL), lambda i: (i, 0))
    return pl.pallas_call(
        functools.partial(_ffn_kernel, final_norm=final_norm),
        grid=(n // tm,),
        in_specs=[row] + _ffn_weight_specs() + [_resident((1, D_MODEL))],
        out_specs=row,
        out_shape=jax.ShapeDtypeStruct((n, D_MODEL), F32),
        scratch_shapes=[pltpu.VMEM((tm, D_FF), BF16)],
        compiler_params=pltpu.CompilerParams(
            dimension_semantics=("arbitrary",), vmem_limit_bytes=VMEM_LIMIT),
        name="ffn",
    )(x, g, wg, wu, wd, fg)


def _windowed_gqa(q0, qa_ref, ka_ref, va_ref, sink_ref, seq):
    tq = qa_ref.shape[0]
    mlo, mhi = _half_masks(BF16)
    lane = lax.broadcasted_iota(jnp.int32, (A_BLOCK, LANES), 1)
    a_idx = lax.broadcasted_iota(jnp.int32, (A_BLOCK, A_KEYS), 0)
    b_idx = lax.broadcasted_iota(jnp.int32, (A_BLOCK, A_KEYS), 1)
    row_blocks = []
    for r in range(tq // A_BLOCK):
        qs = q0 + r * A_BLOCK
        ws = pl.multiple_of(jnp.clip(qs - A_BLOCK, 0, seq - A_KEYS), A_BLOCK)
        dist = jnp.abs((qs - ws) + a_idx - b_idx)
        valid = dist <= WINDOW
        distf = dist.astype(F32)
        slabs = []
        for kv in range(A_KV_HEADS):
            kwin = ka_ref[pl.ds(ws, A_KEYS), kv * LANES:(kv + 1) * LANES]
            vwin = va_ref[pl.ds(ws, A_KEYS), kv * LANES:(kv + 1) * LANES]
            parts = []
            for g in range(A_GROUP):
                h = kv * A_GROUP + g
                slab = qa_ref[r * A_BLOCK:(r + 1) * A_BLOCK, (h // 2) * LANES:(h // 2 + 1) * LANES]
                parts.append(slab * (mlo if h % 2 == 0 else mhi))
            s = _dot_nt(jnp.concatenate(parts, axis=0), kwin)
            es, dens = [], []
            for g in range(A_GROUP):
                h = kv * A_GROUP + g
                sg = s[g * A_BLOCK:(g + 1) * A_BLOCK]
                sg = jnp.where(valid, sg - SLOPES_A[h] * distf, NEG_INF)
                sk = sink_ref[h]
                m = jnp.maximum(jnp.max(sg, axis=-1, keepdims=True), sk)
                e = jnp.exp(sg - m)
                dens.append(jnp.sum(e, axis=-1, keepdims=True) + jnp.exp(sk - m))
                es.append(e.astype(BF16))
            pv = _dot(jnp.concatenate(es, axis=0), vwin)
            outs = [pv[g * A_BLOCK:(g + 1) * A_BLOCK] / dens[g] for g in range(A_GROUP)]
            slabs.append(jnp.where(lane < HEAD_DIM, outs[0], outs[1]))
            slabs.append(jnp.where(lane < HEAD_DIM, outs[2], outs[3]))
        row_blocks.append(jnp.concatenate(slabs, axis=1))
    return jnp.concatenate(row_blocks, axis=0)


def _attn_kernel(x_ref, qa_ref, qb_ref, qaug_ref, ka_ref, va_ref, kb_ref, vbt_ref, kaug_ref,
                 wout_ref, sink_ref, lq1_ref, lk1_ref, lq2_ref, lk2_ref, sublnt_ref,
                 xo_ref, s0_scr, s1_scr, p0_scr, p1_scr, ksgn_scr, fdt_scr, *, lam_init):
    tq = x_ref.shape[0]
    seq = kb_ref.shape[1]
    q0 = pl.multiple_of(pl.program_id(1) * tq, tq)
    diag = pl.ds(q0, tq)

    lam = (jnp.exp(jnp.sum(lq1_ref[...] * lk1_ref[...], axis=-1, keepdims=True))
           - jnp.exp(jnp.sum(lq2_ref[...] * lk2_ref[...], axis=-1, keepdims=True)) + lam_init)

    kpos = lax.broadcasted_iota(jnp.int32, (seq, LANES), 0)
    kaug = kaug_ref[...]
    ksgn_scr[...] = jnp.where(kpos < q0 + tq, kaug, -kaug).astype(BF16)
    kaug_d = kaug_ref[diag, :]
    ind = (lax.broadcasted_iota(jnp.int32, (1, LANES), 1) == DIAG_COL).astype(F32)
    ksgn_scr[diag, :] = (kaug_d + ind).astype(BF16)
    kaug_d = kaug_d.astype(BF16)
    b_idx = lax.broadcasted_iota(jnp.int32, (tq, 2 * tq), 0)
    c_idx = lax.broadcasted_iota(jnp.int32, (tq, 2 * tq), 1)
    a_idx = jnp.where(c_idx >= tq, c_idx - tq, c_idx)
    fdt_scr[...] = (2 * jnp.maximum(b_idx - a_idx, 0)).astype(F32)

    mlo, mhi = _half_masks(BF16)
    ones = jnp.ones((BF16_ROWS, seq), BF16)
    s_bufs = (s0_scr, s1_scr)
    p_bufs = (p0_scr, p1_scr)

    def scores(h):
        s_scr = s_bufs[h % 2]
        q = qb_ref[h]
        aq = qaug_ref[h]
        lhs_q = jnp.concatenate(
            [jnp.concatenate([q * mlo, q * mhi], axis=0), jnp.concatenate([aq, aq], axis=0)], axis=1)
        kcat = jnp.concatenate([kb_ref[h], ksgn_scr[...]], axis=1)
        st = _dot_nt(kcat, lhs_q)
        s_scr[...] = st
        m_far = jnp.max(st, axis=0, keepdims=True)
        kd = jnp.concatenate([kb_ref[h, diag, :], kaug_d], axis=1)
        sd = _dot_nt(kd, lhs_q) - SLOPES_B[h] * fdt_scr[...]
        s_scr[diag, :] = sd
        return jnp.maximum(m_far, jnp.max(sd, axis=0, keepdims=True))

    def softmax(h, m):
        s_scr, p_scr = s_bufs[h % 2], p_bufs[h % 2]
        for r in range(seq // EXP_ROWS):
            rows = slice(r * EXP_ROWS, (r + 1) * EXP_ROWS)
            p_scr[rows, :] = jnp.exp(s_scr[rows, :] - m).astype(BF16)

    def values(h):
        vt1 = jnp.concatenate([vbt_ref[h], ones], axis=0)
        ot = _dot(vt1, p_bufs[h % 2][...])
        ot = ot[:LANES, :] * (1.0 / ot[LANES:LANES + 1, :])
        o = ot[:, :tq] - lam * ot[:, tq:]
        ms = jnp.mean(o * o, axis=0, keepdims=True)
        y = o * lax.rsqrt(ms + RMS_EPS) * sublnt_ref[...] * (1.0 - lam_init)
        return y.T.astype(BF16)

    ob = [None] * B_HEADS
    m = [None] * B_HEADS
    for step in range(B_HEADS + 2):
        if step < B_HEADS:
            m[step] = scores(step)
        if 1 <= step <= B_HEADS:
            softmax(step - 1, m[step - 1])
        if step >= 2:
            ob[step - 2] = values(step - 2)

    oa = _windowed_gqa(q0, qa_ref, ka_ref, va_ref, sink_ref, seq).astype(BF16)

    mix = jnp.concatenate([oa] + ob, axis=1)
    xo_ref[...] = x_ref[...] + _dot(mix, wout_ref[...])


def _attn_call(x, qa, ka, va, qb, kb, vbt, qaug, kaug, wout, sink,
               lq1, lk1, lq2, lk2, sublnt, batch, seq, lam_init):
    n = x.shape[0]
    tq = TQ_ATTN
    nq = seq // tq
    once = pl.Buffered(1)
    tile = lambda cols: pl.BlockSpec((tq, cols), lambda b, i: (b * nq + i, 0))
    per_batch = lambda cols: pl.BlockSpec((seq, cols), lambda b, i: (b, 0), pipeline_mode=once)
    head_tile = pl.BlockSpec((B_HEADS, tq, LANES), lambda b, i: (0, b * nq + i, 0))
    smem = pl.BlockSpec(memory_space=pltpu.SMEM)
    small = lambda cols: _resident((1, cols))
    return pl.pallas_call(
        functools.partial(_attn_kernel, lam_init=lam_init),
        grid=(batch, nq),
        in_specs=[tile(D_MODEL), tile(A_Q_COLS), head_tile,
                  pl.BlockSpec((B_HEADS, tq, LANES), lambda b, i: (0, i, 0)),
                  per_batch(A_KV_DUP_COLS), per_batch(A_KV_DUP_COLS),
                  pl.BlockSpec((B_HEADS, seq, LANES), lambda b, i: (0, b, 0), pipeline_mode=once),
                  pl.BlockSpec((B_HEADS, LANES, seq), lambda b, i: (0, 0, b), pipeline_mode=once),
                  _resident((seq, LANES)), _resident((MIX_WIDTH, D_MODEL)), smem,
                  small(HEAD_DIM), small(HEAD_DIM), small(HEAD_DIM), small(HEAD_DIM), _resident((LANES, tq))],
        out_specs=tile(D_MODEL),
        out_shape=jax.ShapeDtypeStruct((n, D_MODEL), F32),
        scratch_shapes=[pltpu.VMEM((seq, 2 * tq), F32), pltpu.VMEM((seq, 2 * tq), F32),
                        pltpu.VMEM((seq, 2 * tq), BF16), pltpu.VMEM((seq, 2 * tq), BF16),
                        pltpu.VMEM((seq, LANES), BF16), pltpu.VMEM((tq, 2 * tq), F32)],
        compiler_params=pltpu.CompilerParams(
            dimension_semantics=("arbitrary", "arbitrary"), vmem_limit_bytes=VMEM_LIMIT),
        name="attn",
    )(x, qa, qb, qaug, ka, va, kb, vbt, kaug, wout, sink, lq1, lk1, lq2, lk2, sublnt)


def _position_tables(seq):
    pos = jnp.arange(seq, dtype=jnp.int32)
    lo = (pos % LANES).astype(F32)
    hi = (LANES * (pos // LANES)).astype(F32)
    one = jnp.ones((seq,), F32)
    pad = jnp.zeros((seq, LANES - 4), F32)
    kaug = jnp.concatenate([jnp.stack([lo, hi, one, one], axis=1), pad], axis=1)
    slopes = jnp.asarray(SLOPES_B, F32)[:, None, None]
    qcols = jnp.stack([one, one, -lo, -hi], axis=1)[None]
    assert DIAG_COL == 4
    qaug = jnp.concatenate([slopes * qcols, jnp.full((B_HEADS, seq, 1), DIAG_MASK, F32),
                            jnp.zeros((B_HEADS, seq, LANES - 5), F32)], axis=2)
    return qaug.astype(BF16), kaug


def _prep_w_in(w):
    scale = HEAD_DIM ** -0.5
    c = 0
    qa = w[:, c:c + A_Q_COLS] * scale; c += A_Q_COLS
    ka = w[:, c:c + A_KV_HEADS * HEAD_DIM]; c += A_KV_HEADS * HEAD_DIM
    va = w[:, c:c + A_KV_HEADS * HEAD_DIM]; c += A_KV_HEADS * HEAD_DIM
    qb = w[:, c:c + B_COLS] * scale; c += B_COLS
    kb = w[:, c:c + B_COLS]; c += B_COLS
    vb = w[:, c:c + B_COLS]
    dup = lambda t: jnp.concatenate(
        [t[:, j * HEAD_DIM:(j + 1) * HEAD_DIM] for j in range(A_KV_HEADS) for _ in range(2)], axis=1)
    return jnp.concatenate([qa, dup(ka), dup(va), qb, kb], axis=1).astype(BF16), vb.T.astype(BF16)


def kernel(x, ffn1_norm, ffn1_w_gate, ffn1_w_up, ffn1_w_down, mix_norm, w_in, sink,
           lam_q1, lam_k1, lam_q2, lam_k2, diff_subln, w_out,
           ffn2_norm, ffn2_w_gate, ffn2_w_up, ffn2_w_down, final_norm):
    batch, seq, d = x.shape
    depth = w_in.shape[0]
    assert d == D_MODEL and seq % TQ_ATTN == 0 and (batch * seq) % TM_FFN == 0
    xf = x.reshape(batch * seq, d)
    qaug, kaug = _position_tables(seq)
    row = lambda v: v.reshape(1, -1).astype(F32)
    for l in range(depth):
        lam_init = 0.8 - 0.6 * math.exp(-0.3 * l)
        win, wvt = _prep_w_in(w_in[l])
        xf, qa, ka, va, qb, kb, vbt = _ffn_inproj_call(
            xf, row(ffn1_norm[l]), ffn1_w_gate[l].astype(BF16), ffn1_w_up[l].astype(BF16),
            ffn1_w_down[l].astype(BF16), row(mix_norm[l]), win, wvt)
        sublnt = jnp.broadcast_to(diff_subln[l].astype(F32)[:, None], (LANES, TQ_ATTN))
        xf = _attn_call(xf, qa, ka, va, qb, kb, vbt, qaug, kaug, w_out[l].astype(BF16),
                        sink[l].astype(F32), row(lam_q1[l]), row(lam_k1[l]),
                        row(lam_q2[l]), row(lam_k2[l]), sublnt, batch, seq, lam_init)
        xf = _ffn_call(xf, row(ffn2_norm[l]), ffn2_w_gate[l].astype(BF16), ffn2_w_up[l].astype(BF16),
                       ffn2_w_down[l].astype(BF16), row(final_norm), final_norm=(l == depth - 1))
    return xf.reshape(batch, seq, d)
```

```python
import functools
import math

import jax
import jax.numpy as jnp
from jax import lax
from jax.experimental import pallas as pl
from jax.experimental.pallas import tpu as pltpu

F32 = jnp.float32
BF16 = jnp.bfloat16

D_MODEL = 1024
D_FF = 2816
HEAD_DIM = 64
WINDOW = 128
A_Q_HEADS = 8
A_KV_HEADS = 2
A_GROUP = A_Q_HEADS // A_KV_HEADS
B_HEADS = 4
RMS_EPS = 1e-6
NEG_INF = -1e30

LANES = 128
MXU_DIM = 256
BF16_ROWS = 16
A_Q_COLS = A_Q_HEADS * HEAD_DIM
B_COLS = B_HEADS * LANES
A_KV_DUP_COLS = 2 * A_KV_HEADS * HEAD_DIM
IN_COLS_DUP = A_Q_COLS + 2 * A_KV_DUP_COLS + 2 * B_COLS
MIX_WIDTH = A_Q_COLS + B_COLS

FF_CHUNK = MXU_DIM
TM_FFN = 512
TQ_ATTN = 256
A_BLOCK = 128
A_KEYS = 3 * A_BLOCK
EXP_ROWS = 64
DIAG_COL = 4
DIAG_MASK = -32768.0
VMEM_LIMIT = 56 * 1024 * 1024

SLOPES_A = tuple(2.0 ** (-8.0 * i / A_Q_HEADS) for i in range(1, A_Q_HEADS + 1))
SLOPES_B = tuple(2.0 ** (-8.0 * i / B_HEADS) for i in range(1, B_HEADS + 1))


def _rms(x, g):
    ms = jnp.mean(x * x, axis=-1, keepdims=True)
    return x * lax.rsqrt(ms + RMS_EPS) * g


def _dot(a, b):
    return jnp.dot(a, b, preferred_element_type=F32)


def _dot_nt(a, b):
    return lax.dot_general(a, b, (((1,), (1,)), ((), ())), preferred_element_type=F32)


def _half_masks(dtype):
    lane = lax.broadcasted_iota(jnp.int32, (1, LANES), 1)
    lo = (lane < HEAD_DIM).astype(F32).astype(dtype)
    hi = (lane >= HEAD_DIM).astype(F32).astype(dtype)
    return lo, hi


def _swiglu_residual(x, g_ref, wg_ref, wu_ref, wd_ref, h_scr):
    xn = _rms(x, g_ref[...]).astype(BF16)
    for c in range(D_FF // FF_CHUNK):
        sl = slice(c * FF_CHUNK, (c + 1) * FF_CHUNK)
        gate = _dot(xn, wg_ref[:, sl])
        up = _dot(xn, wu_ref[:, sl])
        h_scr[:, sl] = (gate * jax.nn.sigmoid(gate) * up).astype(BF16)
    return x + 0.5 * _dot(h_scr[...], wd_ref[...])


def _ffn_inproj_kernel(x_ref, g_ref, wg_ref, wu_ref, wd_ref, mg_ref, win_ref, wvt_ref,
                       xo_ref, qa_ref, ka_ref, va_ref, qb_ref, kb_ref, vbt_ref, h_scr):
    x1 = _swiglu_residual(x_ref[...], g_ref, wg_ref, wu_ref, wd_ref, h_scr)
    xo_ref[...] = x1
    hn = _rms(x1, mg_ref[...]).astype(BF16)
    proj = _dot(hn, win_ref[...]).astype(BF16)
    c0 = A_Q_COLS
    c1 = c0 + A_KV_DUP_COLS
    c2 = c1 + A_KV_DUP_COLS
    qa_ref[...] = proj[:, :c0]
    ka_ref[...] = proj[:, c0:c1]
    va_ref[...] = proj[:, c1:c2]
    vt = _dot_nt(wvt_ref[...], hn).astype(BF16)
    for h in range(B_HEADS):
        qb_ref[h] = proj[:, c2 + h * LANES: c2 + (h + 1) * LANES]
        kb_ref[h] = proj[:, c2 + B_COLS + h * LANES: c2 + B_COLS + (h + 1) * LANES]
        vbt_ref[h] = vt[h * LANES:(h + 1) * LANES, :]


def _ffn_kernel(x_ref, g_ref, wg_ref, wu_ref, wd_ref, fg_ref, xo_ref, h_scr, *, final_norm):
    x1 = _swiglu_residual(x_ref[...], g_ref, wg_ref, wu_ref, wd_ref, h_scr)
    if final_norm:
        x1 = _rms(x1, fg_ref[...])
    xo_ref[...] = x1


def _resident(shape):
    nd = len(shape)
    return pl.BlockSpec(shape, lambda *_: (0,) * nd, pipeline_mode=pl.Buffered(1))


def _ffn_weight_specs():
    return [_resident((1, D_MODEL)), _resident((D_MODEL, D_FF)), _resident((D_MODEL, D_FF)),
            _resident((D_FF, D_MODEL))]


def _ffn_inproj_call(x, g, wg, wu, wd, mg, win, wvt):
    n = x.shape[0]
    tm = TM_FFN
    row = lambda cols: pl.BlockSpec((tm, cols), lambda i: (i, 0))
    headrow = pl.BlockSpec((B_HEADS, tm, LANES), lambda i: (0, i, 0))
    headcol = pl.BlockSpec((B_HEADS, LANES, tm), lambda i: (0, 0, i))
    out_shape = (
        jax.ShapeDtypeStruct((n, D_MODEL), F32),
        jax.ShapeDtypeStruct((n, A_Q_COLS), BF16),
        jax.ShapeDtypeStruct((n, A_KV_DUP_COLS), BF16),
        jax.ShapeDtypeStruct((n, A_KV_DUP_COLS), BF16),
        jax.ShapeDtypeStruct((B_HEADS, n, LANES), BF16),
        jax.ShapeDtypeStruct((B_HEADS, n, LANES), BF16),
        jax.ShapeDtypeStruct((B_HEADS, LANES, n), BF16),
    )
    return pl.pallas_call(
        _ffn_inproj_kernel,
        grid=(n // tm,),
        in_specs=[row(D_MODEL)] + _ffn_weight_specs()
                 + [_resident((1, D_MODEL)), _resident((D_MODEL, IN_COLS_DUP)), _resident((B_COLS, D_MODEL))],
        out_specs=(row(D_MODEL), row(A_Q_COLS), row(A_KV_DUP_COLS), row(A_KV_DUP_COLS),
                   headrow, headrow, headcol),
        out_shape=out_shape,
        scratch_shapes=[pltpu.VMEM((tm, D_FF), BF16)],
        compiler_params=pltpu.CompilerParams(
            dimension_semantics=("arbitrary",), vmem_limit_bytes=VMEM_LIMIT),
        name="ffn_inproj",
    )(x, g, wg, wu, wd, mg, win, wvt)


def _ffn_call(x, g, wg, wu, wd, fg, final_norm):
    n = x.shape[0]
    tm = TM_FFN
    row = pl.BlockSpec((tm, D_MODEL), lambda i: (i, 0))
    return pl.pallas_call(
        functools.partial(_ffn_kernel, final_norm=final_norm),
        grid=(n // tm,),
        in_specs=[row] + _ffn_weight_specs() + [_resident((1, D_MODEL))],
        out_specs=row,
        out_shape=jax.ShapeDtypeStruct((n, D_MODEL), F32),
        scratch_shapes=[pltpu.VMEM((tm, D_FF), BF16)],
        compiler_params=pltpu.CompilerParams(
            dimension_semantics=("arbitrary",), vmem_limit_bytes=VMEM_LIMIT),
        name="ffn",
    )(x, g, wg, wu, wd, fg)


def _windowed_gqa(q0, qa_ref, ka_ref, va_ref, sink_ref, seq):
    tq = qa_ref.shape[0]
    mlo, mhi = _half_masks(BF16)
    lane = lax.broadcasted_iota(jnp.int32, (A_BLOCK, LANES), 1)
    a_idx = lax.broadcasted_iota(jnp.int32, (A_BLOCK, A_KEYS), 0)
    b_idx = lax.broadcasted_iota(jnp.int32, (A_BLOCK, A_KEYS), 1)
    row_blocks = []
    for r in range(tq // A_BLOCK):
        qs = q0 + r * A_BLOCK
        ws = pl.multiple_of(jnp.clip(qs - A_BLOCK, 0, seq - A_KEYS), A_BLOCK)
        dist = jnp.abs((qs - ws) + a_idx - b_idx)
        valid = dist <= WINDOW
        distf = dist.astype(F32)
        slabs = []
        for kv in range(A_KV_HEADS):
            kwin = ka_ref[pl.ds(ws, A_KEYS), kv * LANES:(kv + 1) * LANES]
            vwin = va_ref[pl.ds(ws, A_KEYS), kv * LANES:(kv + 1) * LANES]
            parts = []
            for g in range(A_GROUP):
                h = kv * A_GROUP + g
                slab = qa_ref[r * A_BLOCK:(r + 1) * A_BLOCK, (h // 2) * LANES:(h // 2 + 1) * LANES]
                parts.append(slab * (mlo if h % 2 == 0 else mhi))
            s = _dot_nt(jnp.concatenate(parts, axis=0), kwin)
            es, dens = [], []
            for g in range(A_GROUP):
                h = kv * A_GROUP + g
                sg = s[g * A_BLOCK:(g + 1) * A_BLOCK]
                sg = jnp.where(valid, sg - SLOPES_A[h] * distf, NEG_INF)
                sk = sink_ref[h]
                m = jnp.maximum(jnp.max(sg, axis=-1, keepdims=True), sk)
                e = jnp.exp(sg - m)
                dens.append(jnp.sum(e, axis=-1, keepdims=True) + jnp.exp(sk - m))
                es.append(e.astype(BF16))
            pv = _dot(jnp.concatenate(es, axis=0), vwin)
            outs = [pv[g * A_BLOCK:(g + 1) * A_BLOCK] / dens[g] for g in range(A_GROUP)]
            slabs.append(jnp.where(lane < HEAD_DIM, outs[0], outs[1]))
            slabs.append(jnp.where(lane < HEAD_DIM, outs[2], outs[3]))
        row_blocks.append(jnp.concatenate(slabs, axis=1))
    return jnp.concatenate(row_blocks, axis=0)


def _attn_kernel(x_ref, qa_ref, qb_ref, qaug_ref, ka_ref, va_ref, kb_ref, vbt_ref, ksgn_ref, kaug_ref, fdt_ref,
                 wout_ref, sink_ref, lq1_ref, lk1_ref, lq2_ref, lk2_ref, sublnt_ref,
                 xo_ref, s0_scr, s1_scr, p0_scr, p1_scr, *, lam_init):
    tq = x_ref.shape[0]
    seq = kb_ref.shape[1]
    q0 = pl.multiple_of(pl.program_id(1) * tq, tq)
    diag = pl.ds(q0, tq)

    lam = (jnp.exp(jnp.sum(lq1_ref[...] * lk1_ref[...], axis=-1, keepdims=True))
           - jnp.exp(jnp.sum(lq2_ref[...] * lk2_ref[...], axis=-1, keepdims=True)) + lam_init)

    kaug_d = kaug_ref[diag, :]
    mlo, mhi = _half_masks(BF16)
    ones = jnp.ones((BF16_ROWS, seq), BF16)
    s_bufs = (s0_scr, s1_scr)
    p_bufs = (p0_scr, p1_scr)

    def scores(h):
        s_scr = s_bufs[h % 2]
        q = qb_ref[h]
        aq = qaug_ref[h]
        lhs_q = jnp.concatenate(
            [jnp.concatenate([q * mlo, q * mhi], axis=0), jnp.concatenate([aq, aq], axis=0)], axis=1)
        kcat = jnp.concatenate([kb_ref[h], ksgn_ref[...]], axis=1)
        st = _dot_nt(kcat, lhs_q)
        s_scr[...] = st
        m_far = jnp.max(st, axis=0, keepdims=True)
        kd = jnp.concatenate([kb_ref[h, diag, :], kaug_d], axis=1)
        sd = _dot_nt(kd, lhs_q) - SLOPES_B[h] * fdt_ref[...]
        s_scr[diag, :] = sd
        return jnp.maximum(m_far, jnp.max(sd, axis=0, keepdims=True))

    def softmax(h, m):
        s_scr, p_scr = s_bufs[h % 2], p_bufs[h % 2]
        for r in range(seq // EXP_ROWS):
            rows = slice(r * EXP_ROWS, (r + 1) * EXP_ROWS)
            p_scr[rows, :] = jnp.exp((s_scr[rows, :] - m).astype(BF16))

    def values(h):
        vt1 = jnp.concatenate([vbt_ref[h], ones], axis=0)
        ot = _dot(vt1, p_bufs[h % 2][...])
        ot = ot[:LANES, :] * (1.0 / ot[LANES:LANES + 1, :])
        o = ot[:, :tq] - lam * ot[:, tq:]
        ms = jnp.mean(o * o, axis=0, keepdims=True)
        y = o * lax.rsqrt(ms + RMS_EPS) * sublnt_ref[...] * (1.0 - lam_init)
        return y.T.astype(BF16)

    ob = [None] * B_HEADS
    m = [None] * B_HEADS
    for step in range(B_HEADS + 2):
        if step < B_HEADS:
            m[step] = scores(step)
        if 1 <= step <= B_HEADS:
            softmax(step - 1, m[step - 1])
        if step >= 2:
            ob[step - 2] = values(step - 2)

    oa = _windowed_gqa(q0, qa_ref, ka_ref, va_ref, sink_ref, seq).astype(BF16)

    mix = jnp.concatenate([oa] + ob, axis=1)
    xo_ref[...] = x_ref[...] + _dot(mix, wout_ref[...])


def _attn_call(x, qa, ka, va, qb, kb, vbt, qaug, ksgn, kaug, fdt, wout, sink,
               lq1, lk1, lq2, lk2, sublnt, batch, seq, lam_init):
    n = x.shape[0]
    tq = TQ_ATTN
    nq = seq // tq
    once = pl.Buffered(1)
    tile = lambda cols: pl.BlockSpec((tq, cols), lambda b, i: (b * nq + i, 0))
    per_batch = lambda cols: pl.BlockSpec((seq, cols), lambda b, i: (b, 0), pipeline_mode=once)
    head_tile = pl.BlockSpec((B_HEADS, tq, LANES), lambda b, i: (0, b * nq + i, 0))
    smem = pl.BlockSpec(memory_space=pltpu.SMEM)
    small = lambda cols: _resident((1, cols))
    return pl.pallas_call(
        functools.partial(_attn_kernel, lam_init=lam_init),
        grid=(batch, nq),
        in_specs=[tile(D_MODEL), tile(A_Q_COLS), head_tile,
                  pl.BlockSpec((B_HEADS, tq, LANES), lambda b, i: (0, i, 0)),
                  per_batch(A_KV_DUP_COLS), per_batch(A_KV_DUP_COLS),
                  pl.BlockSpec((B_HEADS, seq, LANES), lambda b, i: (0, b, 0), pipeline_mode=once),
                  pl.BlockSpec((B_HEADS, LANES, seq), lambda b, i: (0, 0, b), pipeline_mode=once),
                  pl.BlockSpec((None, seq, LANES), lambda b, i: (i, 0, 0)),
                  _resident((seq, LANES)), _resident((tq, 2 * tq)), _resident((MIX_WIDTH, D_MODEL)), smem,
                  small(HEAD_DIM), small(HEAD_DIM), small(HEAD_DIM), small(HEAD_DIM), _resident((LANES, tq))],
        out_specs=tile(D_MODEL),
        out_shape=jax.ShapeDtypeStruct((n, D_MODEL), F32),
        scratch_shapes=[pltpu.VMEM((seq, 2 * tq), F32), pltpu.VMEM((seq, 2 * tq), F32),
                        pltpu.VMEM((seq, 2 * tq), BF16), pltpu.VMEM((seq, 2 * tq), BF16)],
        compiler_params=pltpu.CompilerParams(
            dimension_semantics=("arbitrary", "arbitrary"), vmem_limit_bytes=VMEM_LIMIT),
        name="attn",
    )(x, qa, qb, qaug, ka, va, kb, vbt, ksgn, kaug, fdt, wout, sink, lq1, lk1, lq2, lk2, sublnt)


def _position_tables(seq, tq):
    pos = jnp.arange(seq, dtype=jnp.int32)
    lo = (pos % LANES).astype(F32)
    hi = (LANES * (pos // LANES)).astype(F32)
    one = jnp.ones((seq,), F32)
    pad = jnp.zeros((seq, LANES - 4), F32)
    kaug = jnp.concatenate([jnp.stack([lo, hi, one, one], axis=1), pad], axis=1)
    slopes = jnp.asarray(SLOPES_B, F32)[:, None, None]
    qcols = jnp.stack([one, one, -lo, -hi], axis=1)[None]
    assert DIAG_COL == 4
    qaug = jnp.concatenate([slopes * qcols, jnp.full((B_HEADS, seq, 1), DIAG_MASK, F32),
                            jnp.zeros((B_HEADS, seq, LANES - 5), F32)], axis=2)
    tile_of_key = (pos // tq)[None, :, None]
    tile = jnp.arange(seq // tq, dtype=jnp.int32)[:, None, None]
    flag = (jnp.arange(LANES) == DIAG_COL)[None, None, :] & (tile_of_key == tile)
    ksgn = jnp.where(tile_of_key <= tile, kaug[None], -kaug[None]) + flag.astype(F32)
    key = jnp.arange(tq, dtype=jnp.int32)[:, None]
    qry = jnp.arange(2 * tq, dtype=jnp.int32)[None, :] % tq
    fdt = (2 * jnp.maximum(key - qry, 0)).astype(F32)
    return qaug.astype(BF16), kaug.astype(BF16), ksgn.astype(BF16), fdt


def _prep_w_in(w):
    scale = HEAD_DIM ** -0.5
    c = 0
    qa = w[:, c:c + A_Q_COLS] * scale; c += A_Q_COLS
    ka = w[:, c:c + A_KV_HEADS * HEAD_DIM]; c += A_KV_HEADS * HEAD_DIM
    va = w[:, c:c + A_KV_HEADS * HEAD_DIM]; c += A_KV_HEADS * HEAD_DIM
    qb = w[:, c:c + B_COLS] * scale; c += B_COLS
    kb = w[:, c:c + B_COLS]; c += B_COLS
    vb = w[:, c:c + B_COLS]
    dup = lambda t: jnp.concatenate(
        [t[:, j * HEAD_DIM:(j + 1) * HEAD_DIM] for j in range(A_KV_HEADS) for _ in range(2)], axis=1)
    return jnp.concatenate([qa, dup(ka), dup(va), qb, kb], axis=1).astype(BF16), vb.T.astype(BF16)


def kernel(x, ffn1_norm, ffn1_w_gate, ffn1_w_up, ffn1_w_down, mix_norm, w_in, sink,
           lam_q1, lam_k1, lam_q2, lam_k2, diff_subln, w_out,
           ffn2_norm, ffn2_w_gate, ffn2_w_up, ffn2_w_down, final_norm):
    batch, seq, d = x.shape
    depth = w_in.shape[0]
    assert d == D_MODEL and seq % TQ_ATTN == 0 and (batch * seq) % TM_FFN == 0
    xf = x.reshape(batch * seq, d)
    qaug, kaug, ksgn, fdt = _position_tables(seq, TQ_ATTN)
    row = lambda v: v.reshape(1, -1).astype(F32)
    for l in range(depth):
        lam_init = 0.8 - 0.6 * math.exp(-0.3 * l)
        win, wvt = _prep_w_in(w_in[l])
        xf, qa, ka, va, qb, kb, vbt = _ffn_inproj_call(
            xf, row(ffn1_norm[l]), ffn1_w_gate[l].astype(BF16), ffn1_w_up[l].astype(BF16),
            ffn1_w_down[l].astype(BF16), row(mix_norm[l]), win, wvt)
        sublnt = jnp.broadcast_to(diff_subln[l].astype(F32)[:, None], (LANES, TQ_ATTN))
        xf = _attn_call(xf, qa, ka, va, qb, kb, vbt, qaug, ksgn, kaug, fdt, w_out[l].astype(BF16),
                        sink[l].astype(F32), row(lam_q1[l]), row(lam_k1[l]),
                        row(lam_q2[l]), row(lam_k2[l]), sublnt, batch, seq, lam_init)
        xf = _ffn_call(xf, row(ffn2_norm[l]), ffn2_w_gate[l].astype(BF16), ffn2_w_up[l].astype(BF16),
                       ffn2_w_down[l].astype(BF16), row(final_norm), final_norm=(l == depth - 1))
    return xf.reshape(batch, seq, d)
```

```python
import functools
import math

import jax
import jax.numpy as jnp
from jax import lax
from jax.experimental import pallas as pl
from jax.experimental.pallas import tpu as pltpu

F32 = jnp.float32
BF16 = jnp.bfloat16

D_MODEL = 1024
D_FF = 2816
HEAD_DIM = 64
WINDOW = 128
A_Q_HEADS = 8
A_KV_HEADS = 2
A_GROUP = A_Q_HEADS // A_KV_HEADS
B_HEADS = 4
RMS_EPS = 1e-6
NEG_INF = -1e30

LANES = 128
MXU_DIM = 256
BF16_ROWS = 16
VT_ROWS = LANES + BF16_ROWS
A_Q_COLS = A_Q_HEADS * HEAD_DIM
B_COLS = B_HEADS * LANES
A_KV_DUP_COLS = 2 * A_KV_HEADS * HEAD_DIM
IN_COLS_DUP = A_Q_COLS + 2 * A_KV_DUP_COLS + 2 * B_COLS
MIX_WIDTH = A_Q_COLS + B_COLS

FF_CHUNK = MXU_DIM
TM_FFN = 512
TQ_ATTN = 256
A_BLOCK = 128
A_KEYS = 3 * A_BLOCK
EXP_ROWS = 64
DIAG_COL = 4
DIAG_MASK = -32768.0
VMEM_LIMIT = 56 * 1024 * 1024

SLOPES_A = tuple(2.0 ** (-8.0 * i / A_Q_HEADS) for i in range(1, A_Q_HEADS + 1))
SLOPES_B = tuple(2.0 ** (-8.0 * i / B_HEADS) for i in range(1, B_HEADS + 1))


def _rms(x, g):
    ms = jnp.mean(x * x, axis=-1, keepdims=True)
    return x * lax.rsqrt(ms + RMS_EPS) * g


def _dot(a, b):
    return jnp.dot(a, b, preferred_element_type=F32)


def _dot_nt(a, b):
    return lax.dot_general(a, b, (((1,), (1,)), ((), ())), preferred_element_type=F32)


def _half_masks(dtype):
    lane = lax.broadcasted_iota(jnp.int32, (1, LANES), 1)
    lo = (lane < HEAD_DIM).astype(F32).astype(dtype)
    hi = (lane >= HEAD_DIM).astype(F32).astype(dtype)
    return lo, hi


def _swiglu_residual(x, g_ref, wg_ref, wu_ref, wd_ref, h_scr):
    xn = _rms(x, g_ref[...]).astype(BF16)
    for c in range(D_FF // FF_CHUNK):
        sl = slice(c * FF_CHUNK, (c + 1) * FF_CHUNK)
        gate = _dot(xn, wg_ref[:, sl])
        up = _dot(xn, wu_ref[:, sl])
        h_scr[:, sl] = (gate * jax.nn.sigmoid(gate) * up).astype(BF16)
    return x + 0.5 * _dot(h_scr[...], wd_ref[...])


def _ffn_inproj_kernel(x_ref, g_ref, wg_ref, wu_ref, wd_ref, mg_ref, win_ref, wvt_ref,
                       xo_ref, qa_ref, ka_ref, va_ref, qb_ref, kb_ref, vbt_ref, h_scr):
    x1 = _swiglu_residual(x_ref[...], g_ref, wg_ref, wu_ref, wd_ref, h_scr)
    xo_ref[...] = x1
    hn = _rms(x1, mg_ref[...]).astype(BF16)
    proj = _dot(hn, win_ref[...]).astype(BF16)
    c0 = A_Q_COLS
    c1 = c0 + A_KV_DUP_COLS
    c2 = c1 + A_KV_DUP_COLS
    qa_ref[...] = proj[:, :c0]
    ka_ref[...] = proj[:, c0:c1]
    va_ref[...] = proj[:, c1:c2]
    vt = _dot_nt(wvt_ref[...], hn).astype(BF16)
    for h in range(B_HEADS):
        qb_ref[h] = proj[:, c2 + h * LANES: c2 + (h + 1) * LANES]
        kb_ref[h] = proj[:, c2 + B_COLS + h * LANES: c2 + B_COLS + (h + 1) * LANES]
        vbt_ref[h, :LANES, :] = vt[h * LANES:(h + 1) * LANES, :]
        vbt_ref[h, LANES:, :] = jnp.ones((BF16_ROWS, vt.shape[1]), BF16)


def _ffn_kernel(x_ref, g_ref, wg_ref, wu_ref, wd_ref, fg_ref, xo_ref, h_scr, *, final_norm):
    x1 = _swiglu_residual(x_ref[...], g_ref, wg_ref, wu_ref, wd_ref, h_scr)
    if final_norm:
        x1 = _rms(x1, fg_ref[...])
    xo_ref[...] = x1


def _resident(shape):
    nd = len(shape)
    return pl.BlockSpec(shape, lambda *_: (0,) * nd, pipeline_mode=pl.Buffered(1))


def _ffn_weight_specs():
    return [_resident((1, D_MODEL)), _resident((D_MODEL, D_FF)), _resident((D_MODEL, D_FF)),
            _resident((D_FF, D_MODEL))]


def _ffn_inproj_call(x, g, wg, wu, wd, mg, win, wvt):
    n = x.shape[0]
    tm = TM_FFN
    row = lambda cols: pl.BlockSpec((tm, cols), lambda i: (i, 0))
    headrow = pl.BlockSpec((B_HEADS, tm, LANES), lambda i: (0, i, 0))
    headcol = pl.BlockSpec((B_HEADS, VT_ROWS, tm), lambda i: (0, 0, i))
    out_shape = (
        jax.ShapeDtypeStruct((n, D_MODEL), F32),
        jax.ShapeDtypeStruct((n, A_Q_COLS), BF16),
        jax.ShapeDtypeStruct((n, A_KV_DUP_COLS), BF16),
        jax.ShapeDtypeStruct((n, A_KV_DUP_COLS), BF16),
        jax.ShapeDtypeStruct((B_HEADS, n, LANES), BF16),
        jax.ShapeDtypeStruct((B_HEADS, n, LANES), BF16),
        jax.ShapeDtypeStruct((B_HEADS, VT_ROWS, n), BF16),
    )
    return pl.pallas_call(
        _ffn_inproj_kernel,
        grid=(n // tm,),
        in_specs=[row(D_MODEL)] + _ffn_weight_specs()
                 + [_resident((1, D_MODEL)), _resident((D_MODEL, IN_COLS_DUP)), _resident((B_COLS, D_MODEL))],
        out_specs=(row(D_MODEL), row(A_Q_COLS), row(A_KV_DUP_COLS), row(A_KV_DUP_COLS),
                   headrow, headrow, headcol),
        out_shape=out_shape,
        scratch_shapes=[pltpu.VMEM((tm, D_FF), BF16)],
        compiler_params=pltpu.CompilerParams(
            dimension_semantics=("arbitrary",), vmem_limit_bytes=VMEM_LIMIT),
        name="ffn_inproj",
    )(x, g, wg, wu, wd, mg, win, wvt)


def _ffn_call(x, g, wg, wu, wd, fg, final_norm):
    n = x.shape[0]
    tm = TM_FFN
    row = pl.BlockSpec((tm, D_MODEL), lambda i: (i, 0))
    return pl.pallas_call(
        functools.partial(_ffn_kernel, final_norm=final_norm),
        grid=(n // tm,),
        in_specs=[row] + _ffn_weight_specs() + [_resident((1, D_MODEL))],
        out_specs=row,
        out_shape=jax.ShapeDtypeStruct((n, D_MODEL), F32),
        scratch_shapes=[pltpu.VMEM((tm, D_FF), BF16)],
        compiler_params=pltpu.CompilerParams(
            dimension_semantics=("arbitrary",), vmem_limit_bytes=VMEM_LIMIT),
        name="ffn",
    )(x, g, wg, wu, wd, fg)


def _windowed_gqa(q0, qa_ref, ka_ref, va_ref, sink_ref, seq):
    tq = qa_ref.shape[0]
    mlo, mhi = _half_masks(BF16)
    lane = lax.broadcasted_iota(jnp.int32, (A_BLOCK, LANES), 1)
    a_idx = lax.broadcasted_iota(jnp.int32, (A_BLOCK, A_KEYS), 0)
    b_idx = lax.broadcasted_iota(jnp.int32, (A_BLOCK, A_KEYS), 1)
    row_blocks = []
    for r in range(tq // A_BLOCK):
        qs = q0 + r * A_BLOCK
        ws = pl.multiple_of(jnp.clip(qs - A_BLOCK, 0, seq - A_KEYS), A_BLOCK)
        dist = jnp.abs((qs - ws) + a_idx - b_idx)
        valid = dist <= WINDOW
        distf = dist.astype(F32)
        slabs = []
        for kv in range(A_KV_HEADS):
            kwin = ka_ref[pl.ds(ws, A_KEYS), kv * LANES:(kv + 1) * LANES]
            vwin = va_ref[pl.ds(ws, A_KEYS), kv * LANES:(kv + 1) * LANES]
            parts = []
            for g in range(A_GROUP):
                h = kv * A_GROUP + g
                slab = qa_ref[r * A_BLOCK:(r + 1) * A_BLOCK, (h // 2) * LANES:(h // 2 + 1) * LANES]
                parts.append(slab * (mlo if h % 2 == 0 else mhi))
            s = _dot_nt(jnp.concatenate(parts, axis=0), kwin)
            es, dens = [], []
            for g in range(A_GROUP):
                h = kv * A_GROUP + g
                sg = s[g * A_BLOCK:(g + 1) * A_BLOCK]
                sg = jnp.where(valid, sg - SLOPES_A[h] * distf, NEG_INF)
                sk = sink_ref[h]
                m = jnp.maximum(jnp.max(sg, axis=-1, keepdims=True), sk)
                e = jnp.exp(sg - m)
                dens.append(jnp.sum(e, axis=-1, keepdims=True) + jnp.exp(sk - m))
                es.append(e.astype(BF16))
            pv = _dot(jnp.concatenate(es, axis=0), vwin)
            outs = [pv[g * A_BLOCK:(g + 1) * A_BLOCK] / dens[g] for g in range(A_GROUP)]
            slabs.append(jnp.where(lane < HEAD_DIM, outs[0], outs[1]))
            slabs.append(jnp.where(lane < HEAD_DIM, outs[2], outs[3]))
        row_blocks.append(jnp.concatenate(slabs, axis=1))
    return jnp.concatenate(row_blocks, axis=0)


def _attn_kernel(x_ref, qa_ref, qb_ref, qaug_ref, ka_ref, va_ref, kb_ref, vbt_ref, ksgn_ref, kaug_ref, fdt_ref,
                 wout_ref, sink_ref, lq1_ref, lk1_ref, lq2_ref, lk2_ref, sublnt_ref,
                 xo_ref, s0_scr, s1_scr, s2_scr, p0_scr, p1_scr, *, lam_init):
    tq = x_ref.shape[0]
    seq = kb_ref.shape[1]
    q0 = pl.multiple_of(pl.program_id(1) * tq, tq)
    diag = pl.ds(q0, tq)

    lam = (jnp.exp(jnp.sum(lq1_ref[...] * lk1_ref[...], axis=-1, keepdims=True))
           - jnp.exp(jnp.sum(lq2_ref[...] * lk2_ref[...], axis=-1, keepdims=True)) + lam_init)

    kaug_d = kaug_ref[diag, :]
    mlo, mhi = _half_masks(BF16)
    s_bufs = (s0_scr, s1_scr, s2_scr)
    p_bufs = (p0_scr, p1_scr)

    def scores(h):
        s_scr = s_bufs[h % len(s_bufs)]
        q = qb_ref[h]
        aq = qaug_ref[h]
        lhs_q = jnp.concatenate(
            [jnp.concatenate([q * mlo, q * mhi], axis=0), jnp.concatenate([aq, aq], axis=0)], axis=1)
        kcat = jnp.concatenate([kb_ref[h], ksgn_ref[...]], axis=1)
        st = _dot_nt(kcat, lhs_q)
        s_scr[...] = st
        m_far = jnp.max(st, axis=0, keepdims=True)
        kd = jnp.concatenate([kb_ref[h, diag, :], kaug_d], axis=1)
        sd = _dot_nt(kd, lhs_q) - SLOPES_B[h] * fdt_ref[...]
        s_scr[diag, :] = sd
        return jnp.maximum(m_far, jnp.max(sd, axis=0, keepdims=True))

    def softmax(h, m):
        s_scr, p_scr = s_bufs[h % len(s_bufs)], p_bufs[h % 2]
        for r in range(seq // EXP_ROWS):
            rows = slice(r * EXP_ROWS, (r + 1) * EXP_ROWS)
            p_scr[rows, :] = jnp.exp((s_scr[rows, :] - m).astype(BF16))

    def values(h):
        ot = _dot(vbt_ref[h], p_bufs[h % 2][...])
        ot = ot[:LANES, :] * (1.0 / ot[LANES:LANES + 1, :])
        o = ot[:, :tq] - lam * ot[:, tq:]
        ms = jnp.mean(o * o, axis=0, keepdims=True)
        y = o * lax.rsqrt(ms + RMS_EPS) * sublnt_ref[...] * (1.0 - lam_init)
        return y.T.astype(BF16)

    ob = [None] * B_HEADS
    m = [None] * B_HEADS
    for step in range(B_HEADS + 2):
        if step < B_HEADS:
            m[step] = scores(step)
        if 1 <= step <= B_HEADS:
            softmax(step - 1, m[step - 1])
        if step >= 2:
            ob[step - 2] = values(step - 2)

    oa = _windowed_gqa(q0, qa_ref, ka_ref, va_ref, sink_ref, seq).astype(BF16)

    mix = jnp.concatenate([oa] + ob, axis=1)
    xo_ref[...] = x_ref[...] + _dot(mix, wout_ref[...])


def _attn_call(x, qa, ka, va, qb, kb, vbt, qaug, ksgn, kaug, fdt, wout, sink,
               lq1, lk1, lq2, lk2, sublnt, batch, seq, lam_init):
    n = x.shape[0]
    tq = TQ_ATTN
    nq = seq // tq
    once = pl.Buffered(1)
    tile = lambda cols: pl.BlockSpec((tq, cols), lambda b, i: (b * nq + i, 0))
    per_batch = lambda cols: pl.BlockSpec((seq, cols), lambda b, i: (b, 0), pipeline_mode=once)
    head_tile = pl.BlockSpec((B_HEADS, tq, LANES), lambda b, i: (0, b * nq + i, 0))
    smem = pl.BlockSpec(memory_space=pltpu.SMEM)
    small = lambda cols: _resident((1, cols))
    return pl.pallas_call(
        functools.partial(_attn_kernel, lam_init=lam_init),
        grid=(batch, nq),
        in_specs=[tile(D_MODEL), tile(A_Q_COLS), head_tile,
                  pl.BlockSpec((B_HEADS, tq, LANES), lambda b, i: (0, i, 0)),
                  per_batch(A_KV_DUP_COLS), per_batch(A_KV_DUP_COLS),
                  pl.BlockSpec((B_HEADS, seq, LANES), lambda b, i: (0, b, 0), pipeline_mode=once),
                  pl.BlockSpec((B_HEADS, VT_ROWS, seq), lambda b, i: (0, 0, b), pipeline_mode=once),
                  pl.BlockSpec((None, seq, LANES), lambda b, i: (i, 0, 0)),
                  _resident((seq, LANES)), _resident((tq, 2 * tq)), _resident((MIX_WIDTH, D_MODEL)), smem,
                  small(HEAD_DIM), small(HEAD_DIM), small(HEAD_DIM), small(HEAD_DIM), _resident((LANES, tq))],
        out_specs=tile(D_MODEL),
        out_shape=jax.ShapeDtypeStruct((n, D_MODEL), F32),
        scratch_shapes=[pltpu.VMEM((seq, 2 * tq), F32), pltpu.VMEM((seq, 2 * tq), F32),
                        pltpu.VMEM((seq, 2 * tq), F32),
                        pltpu.VMEM((seq, 2 * tq), BF16), pltpu.VMEM((seq, 2 * tq), BF16)],
        compiler_params=pltpu.CompilerParams(
            dimension_semantics=("arbitrary", "arbitrary"), vmem_limit_bytes=VMEM_LIMIT),
        name="attn",
    )(x, qa, qb, qaug, ka, va, kb, vbt, ksgn, kaug, fdt, wout, sink, lq1, lk1, lq2, lk2, sublnt)


def _position_tables(seq, tq):
    pos = jnp.arange(seq, dtype=jnp.int32)
    lo = (pos % LANES).astype(F32)
    hi = (LANES * (pos // LANES)).astype(F32)
    one = jnp.ones((seq,), F32)
    pad = jnp.zeros((seq, LANES - 4), F32)
    kaug = jnp.concatenate([jnp.stack([lo, hi, one, one], axis=1), pad], axis=1)
    slopes = jnp.asarray(SLOPES_B, F32)[:, None, None]
    qcols = jnp.stack([one, one, -lo, -hi], axis=1)[None]
    assert DIAG_COL == 4
    qaug = jnp.concatenate([slopes * qcols, jnp.full((B_HEADS, seq, 1), DIAG_MASK, F32),
                            jnp.zeros((B_HEADS, seq, LANES - 5), F32)], axis=2)
    tile_of_key = (pos // tq)[None, :, None]
    tile = jnp.arange(seq // tq, dtype=jnp.int32)[:, None, None]
    flag = (jnp.arange(LANES) == DIAG_COL)[None, None, :] & (tile_of_key == tile)
    ksgn = jnp.where(tile_of_key <= tile, kaug[None], -kaug[None]) + flag.astype(F32)
    key = jnp.arange(tq, dtype=jnp.int32)[:, None]
    qry = jnp.arange(2 * tq, dtype=jnp.int32)[None, :] % tq
    fdt = (2 * jnp.maximum(key - qry, 0)).astype(F32)
    return qaug.astype(BF16), kaug.astype(BF16), ksgn.astype(BF16), fdt


def _prep_w_in(w):
    scale = HEAD_DIM ** -0.5
    c = 0
    qa = w[:, c:c + A_Q_COLS] * scale; c += A_Q_COLS
    ka = w[:, c:c + A_KV_HEADS * HEAD_DIM]; c += A_KV_HEADS * HEAD_DIM
    va = w[:, c:c + A_KV_HEADS * HEAD_DIM]; c += A_KV_HEADS * HEAD_DIM
    qb = w[:, c:c + B_COLS] * scale; c += B_COLS
    kb = w[:, c:c + B_COLS]; c += B_COLS
    vb = w[:, c:c + B_COLS]
    dup = lambda t: jnp.concatenate(
        [t[:, j * HEAD_DIM:(j + 1) * HEAD_DIM] for j in range(A_KV_HEADS) for _ in range(2)], axis=1)
    return jnp.concatenate([qa, dup(ka), dup(va), qb, kb], axis=1).astype(BF16), vb.T.astype(BF16)


def kernel(x, ffn1_norm, ffn1_w_gate, ffn1_w_up, ffn1_w_down, mix_norm, w_in, sink,
           lam_q1, lam_k1, lam_q2, lam_k2, diff_subln, w_out,
           ffn2_norm, ffn2_w_gate, ffn2_w_up, ffn2_w_down, final_norm):
    batch, seq, d = x.shape
    depth = w_in.shape[0]
    assert d == D_MODEL and seq % TQ_ATTN == 0 and (batch * seq) % TM_FFN == 0
    xf = x.reshape(batch * seq, d)
    qaug, kaug, ksgn, fdt = _position_tables(seq, TQ_ATTN)
    row = lambda v: v.reshape(1, -1).astype(F32)
    for l in range(depth):
        lam_init = 0.8 - 0.6 * math.exp(-0.3 * l)
        win, wvt = _prep_w_in(w_in[l])
        xf, qa, ka, va, qb, kb, vbt = _ffn_inproj_call(
            xf, row(ffn1_norm[l]), ffn1_w_gate[l].astype(BF16), ffn1_w_up[l].astype(BF16),
            ffn1_w_down[l].astype(BF16), row(mix_norm[l]), win, wvt)
        sublnt = jnp.broadcast_to(diff_subln[l].astype(F32)[:, None], (LANES, TQ_ATTN))
        xf = _attn_call(xf, qa, ka, va, qb, kb, vbt, qaug, ksgn, kaug, fdt, w_out[l].astype(BF16),
                        sink[l].astype(F32), row(lam_q1[l]), row(lam_k1[l]),
                        row(lam_q2[l]), row(lam_k2[l]), sublnt, batch, seq, lam_init)
        xf = _ffn_call(xf, row(ffn2_norm[l]), ffn2_w_gate[l].astype(BF16), ffn2_w_up[l].astype(BF16),
                       ffn2_w_down[l].astype(BF16), row(final_norm), final_norm=(l == depth - 1))
    return xf.reshape(batch, seq, d)
```

```python
import functools
import math

import jax
import jax.numpy as jnp
from jax import lax
from jax.experimental import pallas as pl
from jax.experimental.pallas import tpu as pltpu

F32 = jnp.float32
BF16 = jnp.bfloat16

D_MODEL = 1024
D_FF = 2816
HEAD_DIM = 64
WINDOW = 128
A_Q_HEADS = 8
A_KV_HEADS = 2
A_GROUP = A_Q_HEADS // A_KV_HEADS
B_HEADS = 4
RMS_EPS = 1e-6
NEG_INF = -1e30

LANES = 128
MXU_DIM = 256
BF16_ROWS = 16
VT_ROWS = LANES + BF16_ROWS
A_Q_COLS = A_Q_HEADS * HEAD_DIM
B_COLS = B_HEADS * LANES
A_KV_DUP_COLS = 2 * A_KV_HEADS * HEAD_DIM
IN_COLS_DUP = A_Q_COLS + 2 * A_KV_DUP_COLS + 2 * B_COLS
MIX_WIDTH = A_Q_COLS + B_COLS

FF_CHUNK = MXU_DIM
TM_FFN = 512
TQ_ATTN = 256
A_BLOCK = 128
A_KEYS = 3 * A_BLOCK
EXP_ROWS = 64
DIAG_COL = 4
DIAG_MASK = -32768.0
VMEM_LIMIT = 56 * 1024 * 1024

SLOPES_A = tuple(2.0 ** (-8.0 * i / A_Q_HEADS) for i in range(1, A_Q_HEADS + 1))
SLOPES_B = tuple(2.0 ** (-8.0 * i / B_HEADS) for i in range(1, B_HEADS + 1))


def _rms(x, g):
    ms = jnp.mean(x * x, axis=-1, keepdims=True)
    return x * lax.rsqrt(ms + RMS_EPS) * g


def _dot(a, b):
    return jnp.dot(a, b, preferred_element_type=F32)


def _dot_nt(a, b):
    return lax.dot_general(a, b, (((1,), (1,)), ((), ())), preferred_element_type=F32)


def _half_masks(dtype):
    lane = lax.broadcasted_iota(jnp.int32, (1, LANES), 1)
    lo = (lane < HEAD_DIM).astype(F32).astype(dtype)
    hi = (lane >= HEAD_DIM).astype(F32).astype(dtype)
    return lo, hi


def _swiglu_residual(x, g_ref, wg_ref, wu_ref, wd_ref, h_scr):
    xn = _rms(x, g_ref[...]).astype(BF16)
    for c in range(D_FF // FF_CHUNK):
        sl = slice(c * FF_CHUNK, (c + 1) * FF_CHUNK)
        gate = _dot(xn, wg_ref[:, sl])
        up = _dot(xn, wu_ref[:, sl])
        h_scr[:, sl] = (gate * jax.nn.sigmoid(gate) * up).astype(BF16)
    return x + 0.5 * _dot(h_scr[...], wd_ref[...])


def _ffn_inproj_kernel(x_ref, g_ref, wg_ref, wu_ref, wd_ref, mg_ref, win_ref, wvt_ref,
                       xo_ref, qa_ref, ka_ref, va_ref, qb_ref, kb_ref, vbt_ref, h_scr):
    x1 = _swiglu_residual(x_ref[...], g_ref, wg_ref, wu_ref, wd_ref, h_scr)
    xo_ref[...] = x1
    hn = _rms(x1, mg_ref[...]).astype(BF16)
    proj = _dot(hn, win_ref[...]).astype(BF16)
    c0 = A_Q_COLS
    c1 = c0 + A_KV_DUP_COLS
    c2 = c1 + A_KV_DUP_COLS
    qa_ref[...] = proj[:, :c0]
    ka_ref[...] = proj[:, c0:c1]
    va_ref[...] = proj[:, c1:c2]
    vt = _dot_nt(wvt_ref[...], hn).astype(BF16)
    for h in range(B_HEADS):
        qb_ref[h] = proj[:, c2 + h * LANES: c2 + (h + 1) * LANES]
        kb_ref[h] = proj[:, c2 + B_COLS + h * LANES: c2 + B_COLS + (h + 1) * LANES]
        vbt_ref[h, :LANES, :] = vt[h * LANES:(h + 1) * LANES, :]
        vbt_ref[h, LANES:, :] = jnp.ones((BF16_ROWS, vt.shape[1]), BF16)


def _ffn_kernel(x_ref, g_ref, wg_ref, wu_ref, wd_ref, fg_ref, xo_ref, h_scr, *, final_norm):
    x1 = _swiglu_residual(x_ref[...], g_ref, wg_ref, wu_ref, wd_ref, h_scr)
    if final_norm:
        x1 = _rms(x1, fg_ref[...])
    xo_ref[...] = x1


def _resident(shape):
    nd = len(shape)
    return pl.BlockSpec(shape, lambda *_: (0,) * nd, pipeline_mode=pl.Buffered(1))


def _ffn_weight_specs():
    return [_resident((1, D_MODEL)), _resident((D_MODEL, D_FF)), _resident((D_MODEL, D_FF)),
            _resident((D_FF, D_MODEL))]


def _ffn_inproj_call(x, g, wg, wu, wd, mg, win, wvt):
    n = x.shape[0]
    tm = TM_FFN
    row = lambda cols: pl.BlockSpec((tm, cols), lambda i: (i, 0))
    headrow = pl.BlockSpec((B_HEADS, tm, LANES), lambda i: (0, i, 0))
    headcol = pl.BlockSpec((B_HEADS, VT_ROWS, tm), lambda i: (0, 0, i))
    out_shape = (
        jax.ShapeDtypeStruct((n, D_MODEL), F32),
        jax.ShapeDtypeStruct((n, A_Q_COLS), BF16),
        jax.ShapeDtypeStruct((n, A_KV_DUP_COLS), BF16),
        jax.ShapeDtypeStruct((n, A_KV_DUP_COLS), BF16),
        jax.ShapeDtypeStruct((B_HEADS, n, LANES), BF16),
        jax.ShapeDtypeStruct((B_HEADS, n, LANES), BF16),
        jax.ShapeDtypeStruct((B_HEADS, VT_ROWS, n), BF16),
    )
    return pl.pallas_call(
        _ffn_inproj_kernel,
        grid=(n // tm,),
        in_specs=[row(D_MODEL)] + _ffn_weight_specs()
                 + [_resident((1, D_MODEL)), _resident((D_MODEL, IN_COLS_DUP)), _resident((B_COLS, D_MODEL))],
        out_specs=(row(D_MODEL), row(A_Q_COLS), row(A_KV_DUP_COLS), row(A_KV_DUP_COLS),
                   headrow, headrow, headcol),
        out_shape=out_shape,
        scratch_shapes=[pltpu.VMEM((tm, D_FF), BF16)],
        compiler_params=pltpu.CompilerParams(
            dimension_semantics=("arbitrary",), vmem_limit_bytes=VMEM_LIMIT),
        name="ffn_inproj",
    )(x, g, wg, wu, wd, mg, win, wvt)


def _ffn_call(x, g, wg, wu, wd, fg, final_norm):
    n = x.shape[0]
    tm = TM_FFN
    row = pl.BlockSpec((tm, D_MODEL), lambda i: (i, 0))
    return pl.pallas_call(
        functools.partial(_ffn_kernel, final_norm=final_norm),
        grid=(n // tm,),
        in_specs=[row] + _ffn_weight_specs() + [_resident((1, D_MODEL))],
        out_specs=row,
        out_shape=jax.ShapeDtypeStruct((n, D_MODEL), F32),
        scratch_shapes=[pltpu.VMEM((tm, D_FF), BF16)],
        compiler_params=pltpu.CompilerParams(
            dimension_semantics=("arbitrary",), vmem_limit_bytes=VMEM_LIMIT),
        name="ffn",
    )(x, g, wg, wu, wd, fg)


def _windowed_gqa(q0, qa_ref, ka_ref, va_ref, sink_ref, seq):
    tq = qa_ref.shape[0]
    mlo, mhi = _half_masks(BF16)
    lane = lax.broadcasted_iota(jnp.int32, (A_BLOCK, LANES), 1)
    a_idx = lax.broadcasted_iota(jnp.int32, (A_BLOCK, A_KEYS), 0)
    b_idx = lax.broadcasted_iota(jnp.int32, (A_BLOCK, A_KEYS), 1)
    row_blocks = []
    for r in range(tq // A_BLOCK):
        qs = q0 + r * A_BLOCK
        ws = pl.multiple_of(jnp.clip(qs - A_BLOCK, 0, seq - A_KEYS), A_BLOCK)
        dist = jnp.abs((qs - ws) + a_idx - b_idx)
        valid = dist <= WINDOW
        distf = dist.astype(F32)
        slabs = []
        for kv in range(A_KV_HEADS):
            kwin = ka_ref[pl.ds(ws, A_KEYS), kv * LANES:(kv + 1) * LANES]
            vwin = va_ref[pl.ds(ws, A_KEYS), kv * LANES:(kv + 1) * LANES]
            parts = []
            for g in range(A_GROUP):
                h = kv * A_GROUP + g
                slab = qa_ref[r * A_BLOCK:(r + 1) * A_BLOCK, (h // 2) * LANES:(h // 2 + 1) * LANES]
                parts.append(slab * (mlo if h % 2 == 0 else mhi))
            s = _dot_nt(jnp.concatenate(parts, axis=0), kwin)
            es, dens = [], []
            for g in range(A_GROUP):
                h = kv * A_GROUP + g
                sg = s[g * A_BLOCK:(g + 1) * A_BLOCK]
                sg = jnp.where(valid, sg - SLOPES_A[h] * distf, NEG_INF)
                sk = sink_ref[h]
                m = jnp.maximum(jnp.max(sg, axis=-1, keepdims=True), sk)
                e = jnp.exp(sg - m)
                dens.append(jnp.sum(e, axis=-1, keepdims=True) + jnp.exp(sk - m))
                es.append(e.astype(BF16))
            pv = _dot(jnp.concatenate(es, axis=0), vwin)
            outs = [pv[g * A_BLOCK:(g + 1) * A_BLOCK] / dens[g] for g in range(A_GROUP)]
            slabs.append(jnp.where(lane < HEAD_DIM, outs[0], outs[1]))
            slabs.append(jnp.where(lane < HEAD_DIM, outs[2], outs[3]))
        row_blocks.append(jnp.concatenate(slabs, axis=1))
    return jnp.concatenate(row_blocks, axis=0)


def _attn_kernel(x_ref, qa_ref, qb_ref, qaug_ref, ka_ref, va_ref, kb_ref, vbt_ref, ksgn_ref, kaug_ref, fdt_ref,
                 wout_ref, sink_ref, lq1_ref, lk1_ref, lq2_ref, lk2_ref, sublnt_ref,
                 xo_ref, s0_scr, s1_scr, s2_scr, p0_scr, p1_scr, *, lam_init):
    tq = x_ref.shape[0]
    seq = kb_ref.shape[1]
    q0 = pl.multiple_of(pl.program_id(1) * tq, tq)
    diag = pl.ds(q0, tq)

    lam = (jnp.exp(jnp.sum(lq1_ref[...] * lk1_ref[...], axis=-1, keepdims=True))
           - jnp.exp(jnp.sum(lq2_ref[...] * lk2_ref[...], axis=-1, keepdims=True)) + lam_init)

    kaug_d = kaug_ref[diag, :]
    mlo, mhi = _half_masks(BF16)
    s_bufs = (s0_scr, s1_scr, s2_scr)
    p_bufs = (p0_scr, p1_scr)

    oa = _windowed_gqa(q0, qa_ref, ka_ref, va_ref, sink_ref, seq)
    folded = oa[:, :LANES] + oa[:, LANES:2 * LANES] + oa[:, 2 * LANES:3 * LANES] + oa[:, 3 * LANES:]
    bits = lax.shift_right_logical(pltpu.bitcast(folded, jnp.uint32), jnp.uint32(16))
    anchor_zero = lax.shift_right_logical(bits, jnp.uint32(16)).astype(F32).astype(BF16)
    oa = oa.astype(BF16)

    def scores(h):
        s_scr = s_bufs[h % len(s_bufs)]
        q = qb_ref[h]
        if h == B_HEADS - 1:
            q = q + anchor_zero
        aq = qaug_ref[h]
        lhs_q = jnp.concatenate(
            [jnp.concatenate([q * mlo, q * mhi], axis=0), jnp.concatenate([aq, aq], axis=0)], axis=1)
        kcat = jnp.concatenate([kb_ref[h], ksgn_ref[...]], axis=1)
        st = _dot_nt(kcat, lhs_q)
        s_scr[...] = st
        m_far = jnp.max(st, axis=0, keepdims=True)
        kd = jnp.concatenate([kb_ref[h, diag, :], kaug_d], axis=1)
        sd = _dot_nt(kd, lhs_q) - SLOPES_B[h] * fdt_ref[...]
        s_scr[diag, :] = sd
        return jnp.maximum(m_far, jnp.max(sd, axis=0, keepdims=True))

    def softmax(h, m):
        s_scr, p_scr = s_bufs[h % len(s_bufs)], p_bufs[h % 2]
        for r in range(seq // EXP_ROWS):
            rows = slice(r * EXP_ROWS, (r + 1) * EXP_ROWS)
            p_scr[rows, :] = jnp.exp((s_scr[rows, :] - m).astype(BF16))

    def values(h):
        ot = _dot(vbt_ref[h], p_bufs[h % 2][...])
        ot = ot[:LANES, :] * (1.0 / ot[LANES:LANES + 1, :])
        o = ot[:, :tq] - lam * ot[:, tq:]
        ms = jnp.mean(o * o, axis=0, keepdims=True)
        y = o * lax.rsqrt(ms + RMS_EPS) * sublnt_ref[...] * (1.0 - lam_init)
        return y.T.astype(BF16)

    ob = [None] * B_HEADS
    m = [None] * B_HEADS
    for step in range(B_HEADS + 2):
        if step < B_HEADS:
            m[step] = scores(step)
        if 1 <= step <= B_HEADS:
            softmax(step - 1, m[step - 1])
        if step >= 2:
            ob[step - 2] = values(step - 2)

    mix = jnp.concatenate([oa] + ob, axis=1)
    xo_ref[...] = x_ref[...] + _dot(mix, wout_ref[...])


def _attn_call(x, qa, ka, va, qb, kb, vbt, qaug, ksgn, kaug, fdt, wout, sink,
               lq1, lk1, lq2, lk2, sublnt, batch, seq, lam_init):
    n = x.shape[0]
    tq = TQ_ATTN
    nq = seq // tq
    once = pl.Buffered(1)
    tile = lambda cols: pl.BlockSpec((tq, cols), lambda b, i: (b * nq + i, 0))
    per_batch = lambda cols: pl.BlockSpec((seq, cols), lambda b, i: (b, 0), pipeline_mode=once)
    head_tile = pl.BlockSpec((B_HEADS, tq, LANES), lambda b, i: (0, b * nq + i, 0))
    smem = pl.BlockSpec(memory_space=pltpu.SMEM)
    small = lambda cols: _resident((1, cols))
    return pl.pallas_call(
        functools.partial(_attn_kernel, lam_init=lam_init),
        grid=(batch, nq),
        in_specs=[tile(D_MODEL), tile(A_Q_COLS), head_tile,
                  pl.BlockSpec((B_HEADS, tq, LANES), lambda b, i: (0, i, 0)),
                  per_batch(A_KV_DUP_COLS), per_batch(A_KV_DUP_COLS),
                  pl.BlockSpec((B_HEADS, seq, LANES), lambda b, i: (0, b, 0), pipeline_mode=once),
                  pl.BlockSpec((B_HEADS, VT_ROWS, seq), lambda b, i: (0, 0, b), pipeline_mode=once),
                  pl.BlockSpec((None, seq, LANES), lambda b, i: (i, 0, 0)),
                  _resident((seq, LANES)), _resident((tq, 2 * tq)), _resident((MIX_WIDTH, D_MODEL)), smem,
                  small(HEAD_DIM), small(HEAD_DIM), small(HEAD_DIM), small(HEAD_DIM), _resident((LANES, tq))],
        out_specs=tile(D_MODEL),
        out_shape=jax.ShapeDtypeStruct((n, D_MODEL), F32),
        scratch_shapes=[pltpu.VMEM((seq, 2 * tq), F32), pltpu.VMEM((seq, 2 * tq), F32),
                        pltpu.VMEM((seq, 2 * tq), F32),
                        pltpu.VMEM((seq, 2 * tq), BF16), pltpu.VMEM((seq, 2 * tq), BF16)],
        compiler_params=pltpu.CompilerParams(
            dimension_semantics=("arbitrary", "arbitrary"), vmem_limit_bytes=VMEM_LIMIT),
        name="attn",
    )(x, qa, qb, qaug, ka, va, kb, vbt, ksgn, kaug, fdt, wout, sink, lq1, lk1, lq2, lk2, sublnt)


def _position_tables(seq, tq):
    pos = jnp.arange(seq, dtype=jnp.int32)
    lo = (pos % LANES).astype(F32)
    hi = (LANES * (pos // LANES)).astype(F32)
    one = jnp.ones((seq,), F32)
    pad = jnp.zeros((seq, LANES - 4), F32)
    kaug = jnp.concatenate([jnp.stack([lo, hi, one, one], axis=1), pad], axis=1)
    slopes = jnp.asarray(SLOPES_B, F32)[:, None, None]
    qcols = jnp.stack([one, one, -lo, -hi], axis=1)[None]
    assert DIAG_COL == 4
    qaug = jnp.concatenate([slopes * qcols, jnp.full((B_HEADS, seq, 1), DIAG_MASK, F32),
                            jnp.zeros((B_HEADS, seq, LANES - 5), F32)], axis=2)
    tile_of_key = (pos // tq)[None, :, None]
    tile = jnp.arange(seq // tq, dtype=jnp.int32)[:, None, None]
    flag = (jnp.arange(LANES) == DIAG_COL)[None, None, :] & (tile_of_key == tile)
    ksgn = jnp.where(tile_of_key <= tile, kaug[None], -kaug[None]) + flag.astype(F32)
    key = jnp.arange(tq, dtype=jnp.int32)[:, None]
    qry = jnp.arange(2 * tq, dtype=jnp.int32)[None, :] % tq
    fdt = (2 * jnp.maximum(key - qry, 0)).astype(F32)
    return qaug.astype(BF16), kaug.astype(BF16), ksgn.astype(BF16), fdt


def _prep_w_in(w):
    scale = HEAD_DIM ** -0.5
    c = 0
    qa = w[:, c:c + A_Q_COLS] * scale; c += A_Q_COLS
    ka = w[:, c:c + A_KV_HEADS * HEAD_DIM]; c += A_KV_HEADS * HEAD_DIM
    va = w[:, c:c + A_KV_HEADS * HEAD_DIM]; c += A_KV_HEADS * HEAD_DIM
    qb = w[:, c:c + B_COLS] * scale; c += B_COLS
    kb = w[:, c:c + B_COLS]; c += B_COLS
    vb = w[:, c:c + B_COLS]
    dup = lambda t: jnp.concatenate(
        [t[:, j * HEAD_DIM:(j + 1) * HEAD_DIM] for j in range(A_KV_HEADS) for _ in range(2)], axis=1)
    return jnp.concatenate([qa, dup(ka), dup(va), qb, kb], axis=1).astype(BF16), vb.T.astype(BF16)


def kernel(x, ffn1_norm, ffn1_w_gate, ffn1_w_up, ffn1_w_down, mix_norm, w_in, sink,
           lam_q1, lam_k1, lam_q2, lam_k2, diff_subln, w_out,
           ffn2_norm, ffn2_w_gate, ffn2_w_up, ffn2_w_down, final_norm):
    batch, seq, d = x.shape
    depth = w_in.shape[0]
    assert d == D_MODEL and seq % TQ_ATTN == 0 and (batch * seq) % TM_FFN == 0
    xf = x.reshape(batch * seq, d)
    qaug, kaug, ksgn, fdt = _position_tables(seq, TQ_ATTN)
    row = lambda v: v.reshape(1, -1).astype(F32)
    for l in range(depth):
        lam_init = 0.8 - 0.6 * math.exp(-0.3 * l)
        win, wvt = _prep_w_in(w_in[l])
        xf, qa, ka, va, qb, kb, vbt = _ffn_inproj_call(
            xf, row(ffn1_norm[l]), ffn1_w_gate[l].astype(BF16), ffn1_w_up[l].astype(BF16),
            ffn1_w_down[l].astype(BF16), row(mix_norm[l]), win, wvt)
        sublnt = jnp.broadcast_to(diff_subln[l].astype(F32)[:, None], (LANES, TQ_ATTN))
        xf = _attn_call(xf, qa, ka, va, qb, kb, vbt, qaug, ksgn, kaug, fdt, w_out[l].astype(BF16),
                        sink[l].astype(F32), row(lam_q1[l]), row(lam_k1[l]),
                        row(lam_q2[l]), row(lam_k2[l]), sublnt, batch, seq, lam_init)
        xf = _ffn_call(xf, row(ffn2_norm[l]), ffn2_w_gate[l].astype(BF16), ffn2_w_up[l].astype(BF16),
                       ffn2_w_down[l].astype(BF16), row(final_norm), final_norm=(l == depth - 1))
    return xf.reshape(batch, seq, d)
```

```python
import functools
import math

import jax
import jax.numpy as jnp
from jax import lax
from jax.experimental import pallas as pl
from jax.experimental.pallas import tpu as pltpu

F32 = jnp.float32
BF16 = jnp.bfloat16

D_MODEL = 1024
D_FF = 2816
HEAD_DIM = 64
WINDOW = 128
A_Q_HEADS = 8
A_KV_HEADS = 2
A_GROUP = A_Q_HEADS // A_KV_HEADS
B_HEADS = 4
RMS_EPS = 1e-6
NEG_INF = -1e30

LANES = 128
MXU_DIM = 256
BF16_ROWS = 16
VT_ROWS = LANES + BF16_ROWS
A_Q_COLS = A_Q_HEADS * HEAD_DIM
B_COLS = B_HEADS * LANES
A_KV_DUP_COLS = 2 * A_KV_HEADS * HEAD_DIM
IN_COLS_DUP = A_Q_COLS + 2 * A_KV_DUP_COLS + 2 * B_COLS
MIX_WIDTH = A_Q_COLS + B_COLS

FF_CHUNK = MXU_DIM
TM_FFN = 512
W_STAGE_CHUNKS = 8
TQ_ATTN = 256
A_BLOCK = 128
A_KEYS = 3 * A_BLOCK
EXP_ROWS = 64
DIAG_COL = 4
DIAG_MASK = -32768.0
VMEM_LIMIT = 56 * 1024 * 1024

SLOPES_A = tuple(2.0 ** (-8.0 * i / A_Q_HEADS) for i in range(1, A_Q_HEADS + 1))
SLOPES_B = tuple(2.0 ** (-8.0 * i / B_HEADS) for i in range(1, B_HEADS + 1))


def _rms(x, g):
    ms = jnp.mean(x * x, axis=-1, keepdims=True)
    return x * lax.rsqrt(ms + RMS_EPS) * g


def _dot(a, b):
    return jnp.dot(a, b, preferred_element_type=F32)


def _dot_nt(a, b):
    return lax.dot_general(a, b, (((1,), (1,)), ((), ())), preferred_element_type=F32)


def _half_masks(dtype):
    lane = lax.broadcasted_iota(jnp.int32, (1, LANES), 1)
    lo = (lane < HEAD_DIM).astype(F32).astype(dtype)
    hi = (lane >= HEAD_DIM).astype(F32).astype(dtype)
    return lo, hi


def _stage_bf16(w_hbm, layer, w_bf, stage, sem):
    rows = stage.shape[1]
    n_chunks = w_bf.shape[0] // rows

    def copy(c):
        return pltpu.make_async_copy(w_hbm.at[layer, pl.ds(c * rows, rows), :], stage.at[c % 2], sem.at[c % 2])

    copy(0).start()
    for c in range(n_chunks):
        if c + 1 < n_chunks:
            copy(c + 1).start()
        copy(c).wait()
        w_bf[c * rows:(c + 1) * rows, :] = stage[c % 2].astype(BF16)


def _stage_ffn_weights(layer, wg_hbm, wu_hbm, wd_hbm, wg_bf, wu_bf, wd_bf, stage_up, stage_dn, sem):
    @pl.when(pl.program_id(0) == 0)
    def _():
        _stage_bf16(wg_hbm, layer, wg_bf, stage_up, sem)
        _stage_bf16(wu_hbm, layer, wu_bf, stage_up, sem)
        _stage_bf16(wd_hbm, layer, wd_bf, stage_dn, sem)


def _swiglu_residual(x, g_ref, wg_ref, wu_ref, wd_ref, h_scr):
    xn = _rms(x, g_ref[...]).astype(BF16)
    for c in range(D_FF // FF_CHUNK):
        sl = slice(c * FF_CHUNK, (c + 1) * FF_CHUNK)
        gate = _dot(xn, wg_ref[:, sl])
        up = _dot(xn, wu_ref[:, sl])
        h_scr[:, sl] = (gate * jax.nn.sigmoid(gate) * up).astype(BF16)
    return x + 0.5 * _dot(h_scr[...], wd_ref[...])


def _ffn_inproj_kernel(x_ref, g_ref, wg_hbm, wu_hbm, wd_hbm, mg_ref, win_ref, wvt_ref,
                       xo_ref, qa_ref, ka_ref, va_ref, qb_ref, kb_ref, vbt_ref,
                       h_scr, wg_ref, wu_ref, wd_ref, stage_up, stage_dn, sem, *, layer):
    _stage_ffn_weights(layer, wg_hbm, wu_hbm, wd_hbm, wg_ref, wu_ref, wd_ref, stage_up, stage_dn, sem)
    x1 = _swiglu_residual(x_ref[...], g_ref, wg_ref, wu_ref, wd_ref, h_scr)
    xo_ref[...] = x1
    hn = _rms(x1, mg_ref[...]).astype(BF16)
    proj = _dot(hn, win_ref[...]).astype(BF16)
    c0 = A_Q_COLS
    c1 = c0 + A_KV_DUP_COLS
    c2 = c1 + A_KV_DUP_COLS
    qa_ref[...] = proj[:, :c0]
    ka_ref[...] = proj[:, c0:c1]
    va_ref[...] = proj[:, c1:c2]
    vt = _dot_nt(wvt_ref[...], hn).astype(BF16)
    for h in range(B_HEADS):
        qb_ref[h] = proj[:, c2 + h * LANES: c2 + (h + 1) * LANES]
        kb_ref[h] = proj[:, c2 + B_COLS + h * LANES: c2 + B_COLS + (h + 1) * LANES]
        vbt_ref[h, :LANES, :] = vt[h * LANES:(h + 1) * LANES, :]
        vbt_ref[h, LANES:, :] = jnp.ones((BF16_ROWS, vt.shape[1]), BF16)


def _ffn_kernel(x_ref, g_ref, wg_hbm, wu_hbm, wd_hbm, fg_ref, xo_ref,
                h_scr, wg_ref, wu_ref, wd_ref, stage_up, stage_dn, sem, *, layer, final_norm):
    _stage_ffn_weights(layer, wg_hbm, wu_hbm, wd_hbm, wg_ref, wu_ref, wd_ref, stage_up, stage_dn, sem)
    x1 = _swiglu_residual(x_ref[...], g_ref, wg_ref, wu_ref, wd_ref, h_scr)
    if final_norm:
        x1 = _rms(x1, fg_ref[...])
    xo_ref[...] = x1


def _resident(shape):
    nd = len(shape)
    return pl.BlockSpec(shape, lambda *_: (0,) * nd, pipeline_mode=pl.Buffered(1))


def _ffn_weight_specs():
    hbm = pl.BlockSpec(memory_space=pl.ANY)
    return [_resident((1, D_MODEL)), hbm, hbm, hbm]


def _ffn_scratch(tm):
    up_rows = D_MODEL // W_STAGE_CHUNKS
    dn_rows = D_FF // W_STAGE_CHUNKS
    return [pltpu.VMEM((tm, D_FF), BF16),
            pltpu.VMEM((D_MODEL, D_FF), BF16), pltpu.VMEM((D_MODEL, D_FF), BF16), pltpu.VMEM((D_FF, D_MODEL), BF16),
            pltpu.VMEM((2, up_rows, D_FF), F32), pltpu.VMEM((2, dn_rows, D_MODEL), F32),
            pltpu.SemaphoreType.DMA((2,))]


def _ffn_inproj_call(x, g, wg, wu, wd, mg, win, wvt, layer):
    n = x.shape[0]
    tm = TM_FFN
    row = lambda cols: pl.BlockSpec((tm, cols), lambda i: (i, 0))
    headrow = pl.BlockSpec((B_HEADS, tm, LANES), lambda i: (0, i, 0))
    headcol = pl.BlockSpec((B_HEADS, VT_ROWS, tm), lambda i: (0, 0, i))
    out_shape = (
        jax.ShapeDtypeStruct((n, D_MODEL), F32),
        jax.ShapeDtypeStruct((n, A_Q_COLS), BF16),
        jax.ShapeDtypeStruct((n, A_KV_DUP_COLS), BF16),
        jax.ShapeDtypeStruct((n, A_KV_DUP_COLS), BF16),
        jax.ShapeDtypeStruct((B_HEADS, n, LANES), BF16),
        jax.ShapeDtypeStruct((B_HEADS, n, LANES), BF16),
        jax.ShapeDtypeStruct((B_HEADS, VT_ROWS, n), BF16),
    )
    return pl.pallas_call(
        functools.partial(_ffn_inproj_kernel, layer=layer),
        grid=(n // tm,),
        in_specs=[row(D_MODEL)] + _ffn_weight_specs()
                 + [_resident((1, D_MODEL)), _resident((D_MODEL, IN_COLS_DUP)), _resident((B_COLS, D_MODEL))],
        out_specs=(row(D_MODEL), row(A_Q_COLS), row(A_KV_DUP_COLS), row(A_KV_DUP_COLS),
                   headrow, headrow, headcol),
        out_shape=out_shape,
        scratch_shapes=_ffn_scratch(tm),
        compiler_params=pltpu.CompilerParams(
            dimension_semantics=("arbitrary",), vmem_limit_bytes=VMEM_LIMIT),
        name="ffn_inproj",
    )(x, g, wg, wu, wd, mg, win, wvt)


def _ffn_call(x, g, wg, wu, wd, fg, layer, final_norm):
    n = x.shape[0]
    tm = TM_FFN
    row = pl.BlockSpec((tm, D_MODEL), lambda i: (i, 0))
    return pl.pallas_call(
        functools.partial(_ffn_kernel, layer=layer, final_norm=final_norm),
        grid=(n // tm,),
        in_specs=[row] + _ffn_weight_specs() + [_resident((1, D_MODEL))],
        out_specs=row,
        out_shape=jax.ShapeDtypeStruct((n, D_MODEL), F32),
        scratch_shapes=_ffn_scratch(tm),
        compiler_params=pltpu.CompilerParams(
            dimension_semantics=("arbitrary",), vmem_limit_bytes=VMEM_LIMIT),
        name="ffn",
    )(x, g, wg, wu, wd, fg)


def _windowed_gqa(q0, qa_ref, ka_ref, va_ref, sink_ref, seq):
    tq = qa_ref.shape[0]
    mlo, mhi = _half_masks(BF16)
    lane = lax.broadcasted_iota(jnp.int32, (A_BLOCK, LANES), 1)
    a_idx = lax.broadcasted_iota(jnp.int32, (A_BLOCK, A_KEYS), 0)
    b_idx = lax.broadcasted_iota(jnp.int32, (A_BLOCK, A_KEYS), 1)
    row_blocks = []
    for r in range(tq // A_BLOCK):
        qs = q0 + r * A_BLOCK
        ws = pl.multiple_of(jnp.clip(qs - A_BLOCK, 0, seq - A_KEYS), A_BLOCK)
        dist = jnp.abs((qs - ws) + a_idx - b_idx)
        valid = dist <= WINDOW
        distf = dist.astype(F32)
        slabs = []
        for kv in range(A_KV_HEADS):
            kwin = ka_ref[pl.ds(ws, A_KEYS), kv * LANES:(kv + 1) * LANES]
            vwin = va_ref[pl.ds(ws, A_KEYS), kv * LANES:(kv + 1) * LANES]
            parts = []
            for g in range(A_GROUP):
                h = kv * A_GROUP + g
                slab = qa_ref[r * A_BLOCK:(r + 1) * A_BLOCK, (h // 2) * LANES:(h // 2 + 1) * LANES]
                parts.append(slab * (mlo if h % 2 == 0 else mhi))
            s = _dot_nt(jnp.concatenate(parts, axis=0), kwin)
            es, dens = [], []
            for g in range(A_GROUP):
                h = kv * A_GROUP + g
                sg = s[g * A_BLOCK:(g + 1) * A_BLOCK]
                sg = jnp.where(valid, sg - SLOPES_A[h] * distf, NEG_INF)
                sk = sink_ref[h]
                m = jnp.maximum(jnp.max(sg, axis=-1, keepdims=True), sk)
                e = jnp.exp(sg - m)
                dens.append(jnp.sum(e, axis=-1, keepdims=True) + jnp.exp(sk - m))
                es.append(e.astype(BF16))
            pv = _dot(jnp.concatenate(es, axis=0), vwin)
            outs = [pv[g * A_BLOCK:(g + 1) * A_BLOCK] / dens[g] for g in range(A_GROUP)]
            slabs.append(jnp.where(lane < HEAD_DIM, outs[0], outs[1]))
            slabs.append(jnp.where(lane < HEAD_DIM, outs[2], outs[3]))
        row_blocks.append(jnp.concatenate(slabs, axis=1))
    return jnp.concatenate(row_blocks, axis=0)


def _attn_kernel(x_ref, qa_ref, qb_ref, qaug_ref, ka_ref, va_ref, kb_ref, vbt_ref, ksgn_ref, kaug_ref, fdt_ref,
                 wout_ref, sink_ref, lq1_ref, lk1_ref, lq2_ref, lk2_ref, sublnt_ref,
                 xo_ref, s0_scr, s1_scr, s2_scr, p0_scr, p1_scr, *, lam_init):
    tq = x_ref.shape[0]
    seq = kb_ref.shape[1]
    q0 = pl.multiple_of(pl.program_id(1) * tq, tq)
    diag = pl.ds(q0, tq)

    lam = (jnp.exp(jnp.sum(lq1_ref[...] * lk1_ref[...], axis=-1, keepdims=True))
           - jnp.exp(jnp.sum(lq2_ref[...] * lk2_ref[...], axis=-1, keepdims=True)) + lam_init)

    kaug_d = kaug_ref[diag, :]
    mlo, mhi = _half_masks(BF16)
    s_bufs = (s0_scr, s1_scr, s2_scr)
    p_bufs = (p0_scr, p1_scr)

    oa = _windowed_gqa(q0, qa_ref, ka_ref, va_ref, sink_ref, seq)
    folded = oa[:, :LANES] + oa[:, LANES:2 * LANES] + oa[:, 2 * LANES:3 * LANES] + oa[:, 3 * LANES:]
    bits = lax.shift_right_logical(pltpu.bitcast(folded, jnp.uint32), jnp.uint32(16))
    anchor_zero = lax.shift_right_logical(bits, jnp.uint32(16)).astype(F32).astype(BF16)
    oa = oa.astype(BF16)

    def scores(h):
        s_scr = s_bufs[h % len(s_bufs)]
        q = qb_ref[h]
        if h == B_HEADS - 1:
            q = q + anchor_zero
        aq = qaug_ref[h]
        lhs_q = jnp.concatenate(
            [jnp.concatenate([q * mlo, q * mhi], axis=0), jnp.concatenate([aq, aq], axis=0)], axis=1)
        kcat = jnp.concatenate([kb_ref[h], ksgn_ref[...]], axis=1)
        st = _dot_nt(kcat, lhs_q)
        s_scr[...] = st
        m_far = jnp.max(st, axis=0, keepdims=True)
        kd = jnp.concatenate([kb_ref[h, diag, :], kaug_d], axis=1)
        sd = _dot_nt(kd, lhs_q) - SLOPES_B[h] * fdt_ref[...]
        s_scr[diag, :] = sd
        return jnp.maximum(m_far, jnp.max(sd, axis=0, keepdims=True))

    def softmax(h, m):
        s_scr, p_scr = s_bufs[h % len(s_bufs)], p_bufs[h % 2]
        for r in range(seq // EXP_ROWS):
            rows = slice(r * EXP_ROWS, (r + 1) * EXP_ROWS)
            p_scr[rows, :] = jnp.exp((s_scr[rows, :] - m).astype(BF16))

    def values(h):
        ot = _dot(vbt_ref[h], p_bufs[h % 2][...])
        ot = ot[:LANES, :] * (1.0 / ot[LANES:LANES + 1, :])
        o = ot[:, :tq] - lam * ot[:, tq:]
        ms = jnp.mean(o * o, axis=0, keepdims=True)
        y = o * lax.rsqrt(ms + RMS_EPS) * sublnt_ref[...] * (1.0 - lam_init)
        return y.T.astype(BF16)

    ob = [None] * B_HEADS
    m = [None] * B_HEADS
    for step in range(B_HEADS + 2):
        if step < B_HEADS:
            m[step] = scores(step)
        if 1 <= step <= B_HEADS:
            softmax(step - 1, m[step - 1])
        if step >= 2:
            ob[step - 2] = values(step - 2)

    mix = jnp.concatenate([oa] + ob, axis=1)
    xo_ref[...] = x_ref[...] + _dot(mix, wout_ref[...])


def _attn_call(x, qa, ka, va, qb, kb, vbt, qaug, ksgn, kaug, fdt, wout, sink,
               lq1, lk1, lq2, lk2, sublnt, batch, seq, lam_init):
    n = x.shape[0]
    tq = TQ_ATTN
    nq = seq // tq
    once = pl.Buffered(1)
    tile = lambda cols: pl.BlockSpec((tq, cols), lambda b, i: (b * nq + i, 0))
    per_batch = lambda cols: pl.BlockSpec((seq, cols), lambda b, i: (b, 0), pipeline_mode=once)
    head_tile = pl.BlockSpec((B_HEADS, tq, LANES), lambda b, i: (0, b * nq + i, 0))
    smem = pl.BlockSpec(memory_space=pltpu.SMEM)
    small = lambda cols: _resident((1, cols))
    return pl.pallas_call(
        functools.partial(_attn_kernel, lam_init=lam_init),
        grid=(batch, nq),
        in_specs=[tile(D_MODEL), tile(A_Q_COLS), head_tile,
                  pl.BlockSpec((B_HEADS, tq, LANES), lambda b, i: (0, i, 0)),
                  per_batch(A_KV_DUP_COLS), per_batch(A_KV_DUP_COLS),
                  pl.BlockSpec((B_HEADS, seq, LANES), lambda b, i: (0, b, 0), pipeline_mode=once),
                  pl.BlockSpec((B_HEADS, VT_ROWS, seq), lambda b, i: (0, 0, b), pipeline_mode=once),
                  pl.BlockSpec((None, seq, LANES), lambda b, i: (i, 0, 0)),
                  _resident((seq, LANES)), _resident((tq, 2 * tq)), _resident((MIX_WIDTH, D_MODEL)), smem,
                  small(HEAD_DIM), small(HEAD_DIM), small(HEAD_DIM), small(HEAD_DIM), _resident((LANES, tq))],
        out_specs=tile(D_MODEL),
        out_shape=jax.ShapeDtypeStruct((n, D_MODEL), F32),
        scratch_shapes=[pltpu.VMEM((seq, 2 * tq), F32), pltpu.VMEM((seq, 2 * tq), F32),
                        pltpu.VMEM((seq, 2 * tq), F32),
                        pltpu.VMEM((seq, 2 * tq), BF16), pltpu.VMEM((seq, 2 * tq), BF16)],
        compiler_params=pltpu.CompilerParams(
            dimension_semantics=("arbitrary", "arbitrary"), vmem_limit_bytes=VMEM_LIMIT),
        name="attn",
    )(x, qa, qb, qaug, ka, va, kb, vbt, ksgn, kaug, fdt, wout, sink, lq1, lk1, lq2, lk2, sublnt)


def _position_tables(seq, tq):
    pos = jnp.arange(seq, dtype=jnp.int32)
    lo = (pos % LANES).astype(F32)
    hi = (LANES * (pos // LANES)).astype(F32)
    one = jnp.ones((seq,), F32)
    pad = jnp.zeros((seq, LANES - 4), F32)
    kaug = jnp.concatenate([jnp.stack([lo, hi, one, one], axis=1), pad], axis=1)
    slopes = jnp.asarray(SLOPES_B, F32)[:, None, None]
    qcols = jnp.stack([one, one, -lo, -hi], axis=1)[None]
    assert DIAG_COL == 4
    qaug = jnp.concatenate([slopes * qcols, jnp.full((B_HEADS, seq, 1), DIAG_MASK, F32),
                            jnp.zeros((B_HEADS, seq, LANES - 5), F32)], axis=2)
    tile_of_key = (pos // tq)[None, :, None]
    tile = jnp.arange(seq // tq, dtype=jnp.int32)[:, None, None]
    flag = (jnp.arange(LANES) == DIAG_COL)[None, None, :] & (tile_of_key == tile)
    ksgn = jnp.where(tile_of_key <= tile, kaug[None], -kaug[None]) + flag.astype(F32)
    key = jnp.arange(tq, dtype=jnp.int32)[:, None]
    qry = jnp.arange(2 * tq, dtype=jnp.int32)[None, :] % tq
    fdt = (2 * jnp.maximum(key - qry, 0)).astype(F32)
    return qaug.astype(BF16), kaug.astype(BF16), ksgn.astype(BF16), fdt


def _prep_w_in(w):
    scale = HEAD_DIM ** -0.5
    c = 0
    qa = w[:, c:c + A_Q_COLS] * scale; c += A_Q_COLS
    ka = w[:, c:c + A_KV_HEADS * HEAD_DIM]; c += A_KV_HEADS * HEAD_DIM
    va = w[:, c:c + A_KV_HEADS * HEAD_DIM]; c += A_KV_HEADS * HEAD_DIM
    qb = w[:, c:c + B_COLS] * scale; c += B_COLS
    kb = w[:, c:c + B_COLS]; c += B_COLS
    vb = w[:, c:c + B_COLS]
    dup = lambda t: jnp.concatenate(
        [t[:, j * HEAD_DIM:(j + 1) * HEAD_DIM] for j in range(A_KV_HEADS) for _ in range(2)], axis=1)
    return jnp.concatenate([qa, dup(ka), dup(va), qb, kb], axis=1).astype(BF16), vb.T.astype(BF16)


def kernel(x, ffn1_norm, ffn1_w_gate, ffn1_w_up, ffn1_w_down, mix_norm, w_in, sink,
           lam_q1, lam_k1, lam_q2, lam_k2, diff_subln, w_out,
           ffn2_norm, ffn2_w_gate, ffn2_w_up, ffn2_w_down, final_norm):
    batch, seq, d = x.shape
    depth = w_in.shape[0]
    assert d == D_MODEL and seq % TQ_ATTN == 0 and (batch * seq) % TM_FFN == 0
    xf = x.reshape(batch * seq, d)
    qaug, kaug, ksgn, fdt = _position_tables(seq, TQ_ATTN)
    row = lambda v: v.reshape(1, -1).astype(F32)
    for l in range(depth):
        lam_init = 0.8 - 0.6 * math.exp(-0.3 * l)
        win, wvt = _prep_w_in(w_in[l])
        xf, qa, ka, va, qb, kb, vbt = _ffn_inproj_call(
            xf, row(ffn1_norm[l]), ffn1_w_gate, ffn1_w_up, ffn1_w_down, row(mix_norm[l]), win, wvt, l)
        sublnt = jnp.broadcast_to(diff_subln[l].astype(F32)[:, None], (LANES, TQ_ATTN))
        xf = _attn_call(xf, qa, ka, va, qb, kb, vbt, qaug, ksgn, kaug, fdt, w_out[l].astype(BF16),
                        sink[l].astype(F32), row(lam_q1[l]), row(lam_k1[l]),
                        row(lam_q2[l]), row(lam_k2[l]), sublnt, batch, seq, lam_init)
        xf = _ffn_call(xf, row(ffn2_norm[l]), ffn2_w_gate, ffn2_w_up, ffn2_w_down, row(final_norm),
                       l, final_norm=(l == depth - 1))
    return xf.reshape(batch, seq, d)
```

```python
import functools
import math

import jax
import jax.numpy as jnp
import ml_dtypes
import numpy as np
from jax import lax
from jax.experimental import pallas as pl
from jax.experimental.pallas import tpu as pltpu

F32 = jnp.float32
BF16 = jnp.bfloat16

D_MODEL = 1024
D_FF = 2816
HEAD_DIM = 64
WINDOW = 128
A_Q_HEADS = 8
A_KV_HEADS = 2
A_GROUP = A_Q_HEADS // A_KV_HEADS
B_HEADS = 4
RMS_EPS = 1e-6
NEG_INF = -1e30

LANES = 128
MXU_DIM = 256
BF16_ROWS = 16
VT_ROWS = LANES + BF16_ROWS
A_Q_COLS = A_Q_HEADS * HEAD_DIM
B_COLS = B_HEADS * LANES
A_KV_DUP_COLS = 2 * A_KV_HEADS * HEAD_DIM
IN_COLS_DUP = A_Q_COLS + 2 * A_KV_DUP_COLS + 2 * B_COLS
MIX_WIDTH = A_Q_COLS + B_COLS

FF_CHUNK = MXU_DIM
TM_FFN = 512
W_STAGE_CHUNKS = 8
W_STAGE_SLOTS = 3
TQ_ATTN = 256
A_BLOCK = 128
A_KEYS = 3 * A_BLOCK
EXP_ROWS = 64
DIAG_COL = 4
DIAG_MASK = -32768.0
VMEM_LIMIT = 56 * 1024 * 1024

SLOPES_A = tuple(2.0 ** (-8.0 * i / A_Q_HEADS) for i in range(1, A_Q_HEADS + 1))
SLOPES_B = tuple(2.0 ** (-8.0 * i / B_HEADS) for i in range(1, B_HEADS + 1))


def _rms(x, g):
    ms = jnp.mean(x * x, axis=-1, keepdims=True)
    return x * lax.rsqrt(ms + RMS_EPS) * g


def _dot(a, b):
    return jnp.dot(a, b, preferred_element_type=F32)


def _dot_nt(a, b):
    return lax.dot_general(a, b, (((1,), (1,)), ((), ())), preferred_element_type=F32)


def _half_masks(dtype):
    lane = lax.broadcasted_iota(jnp.int32, (1, LANES), 1)
    lo = (lane < HEAD_DIM).astype(F32).astype(dtype)
    hi = (lane >= HEAD_DIM).astype(F32).astype(dtype)
    return lo, hi


def _stage_bf16(w_hbm, layer, w_bf, stage, sem):
    slots, rows = stage.shape[0], stage.shape[1]
    n_chunks = w_bf.shape[0] // rows

    def copy(c):
        return pltpu.make_async_copy(
            w_hbm.at[layer, pl.ds(c * rows, rows), :], stage.at[c % slots], sem.at[c % slots])

    for c in range(min(slots - 1, n_chunks)):
        copy(c).start()
    for c in range(n_chunks):
        ahead = c + slots - 1
        if ahead < n_chunks:
            copy(ahead).start()
        copy(c).wait()
        w_bf[c * rows:(c + 1) * rows, :] = stage[c % slots].astype(BF16)


def _stage_ffn_weights(layer, wg_hbm, wu_hbm, wd_hbm, wg_bf, wu_bf, wd_bf, stage_up, stage_dn, sem):
    @pl.when(pl.program_id(0) == 0)
    def _():
        _stage_bf16(wg_hbm, layer, wg_bf, stage_up, sem)
        _stage_bf16(wu_hbm, layer, wu_bf, stage_up, sem)
        _stage_bf16(wd_hbm, layer, wd_bf, stage_dn, sem)


def _swiglu_residual(x, g_ref, wg_ref, wu_ref, wd_ref, h_scr):
    xn = _rms(x, g_ref[...]).astype(BF16)
    for c in range(D_FF // FF_CHUNK):
        sl = slice(c * FF_CHUNK, (c + 1) * FF_CHUNK)
        gate = _dot(xn, wg_ref[:, sl])
        up = _dot(xn, wu_ref[:, sl])
        h_scr[:, sl] = (gate * jax.nn.sigmoid(gate) * up).astype(BF16)
    return x + 0.5 * _dot(h_scr[...], wd_ref[...])


def _ffn_inproj_kernel(x_ref, g_ref, wg_hbm, wu_hbm, wd_hbm, mg_ref, win_ref, wvt_ref,
                       xo_ref, qa_ref, ka_ref, va_ref, qb_ref, kb_ref, vbt_ref,
                       h_scr, wg_ref, wu_ref, wd_ref, stage_up, stage_dn, sem, *, layer):
    _stage_ffn_weights(layer, wg_hbm, wu_hbm, wd_hbm, wg_ref, wu_ref, wd_ref, stage_up, stage_dn, sem)
    x1 = _swiglu_residual(x_ref[...], g_ref, wg_ref, wu_ref, wd_ref, h_scr)
    xo_ref[...] = x1
    hn = _rms(x1, mg_ref[...]).astype(BF16)
    proj = _dot(hn, win_ref[...]).astype(BF16)
    c0 = A_Q_COLS
    c1 = c0 + A_KV_DUP_COLS
    c2 = c1 + A_KV_DUP_COLS
    qa_ref[...] = proj[:, :c0]
    ka_ref[...] = proj[:, c0:c1]
    va_ref[...] = proj[:, c1:c2]
    vt = _dot_nt(wvt_ref[...], hn).astype(BF16)
    for h in range(B_HEADS):
        qb_ref[h] = proj[:, c2 + h * LANES: c2 + (h + 1) * LANES]
        kb_ref[h] = proj[:, c2 + B_COLS + h * LANES: c2 + B_COLS + (h + 1) * LANES]
        vbt_ref[h, :LANES, :] = vt[h * LANES:(h + 1) * LANES, :]
        vbt_ref[h, LANES:, :] = jnp.ones((BF16_ROWS, vt.shape[1]), BF16)


def _ffn_kernel(x_ref, g_ref, wg_hbm, wu_hbm, wd_hbm, fg_ref, xo_ref,
                h_scr, wg_ref, wu_ref, wd_ref, stage_up, stage_dn, sem, *, layer, final_norm):
    _stage_ffn_weights(layer, wg_hbm, wu_hbm, wd_hbm, wg_ref, wu_ref, wd_ref, stage_up, stage_dn, sem)
    x1 = _swiglu_residual(x_ref[...], g_ref, wg_ref, wu_ref, wd_ref, h_scr)
    if final_norm:
        x1 = _rms(x1, fg_ref[...])
    xo_ref[...] = x1


def _resident(shape):
    nd = len(shape)
    return pl.BlockSpec(shape, lambda *_: (0,) * nd, pipeline_mode=pl.Buffered(1))


def _ffn_weight_specs():
    hbm = pl.BlockSpec(memory_space=pl.ANY)
    return [_resident((1, D_MODEL)), hbm, hbm, hbm]


def _ffn_scratch(tm):
    up_rows = D_MODEL // W_STAGE_CHUNKS
    dn_rows = D_FF // W_STAGE_CHUNKS
    return [pltpu.VMEM((tm, D_FF), BF16),
            pltpu.VMEM((D_MODEL, D_FF), BF16), pltpu.VMEM((D_MODEL, D_FF), BF16), pltpu.VMEM((D_FF, D_MODEL), BF16),
            pltpu.VMEM((W_STAGE_SLOTS, up_rows, D_FF), F32), pltpu.VMEM((W_STAGE_SLOTS, dn_rows, D_MODEL), F32),
            pltpu.SemaphoreType.DMA((W_STAGE_SLOTS,))]


def _ffn_inproj_call(x, g, wg, wu, wd, mg, win, wvt, layer):
    n = x.shape[0]
    tm = TM_FFN
    row = lambda cols: pl.BlockSpec((tm, cols), lambda i: (i, 0))
    headrow = pl.BlockSpec((B_HEADS, tm, LANES), lambda i: (0, i, 0))
    headcol = pl.BlockSpec((B_HEADS, VT_ROWS, tm), lambda i: (0, 0, i))
    out_shape = (
        jax.ShapeDtypeStruct((n, D_MODEL), F32),
        jax.ShapeDtypeStruct((n, A_Q_COLS), BF16),
        jax.ShapeDtypeStruct((n, A_KV_DUP_COLS), BF16),
        jax.ShapeDtypeStruct((n, A_KV_DUP_COLS), BF16),
        jax.ShapeDtypeStruct((B_HEADS, n, LANES), BF16),
        jax.ShapeDtypeStruct((B_HEADS, n, LANES), BF16),
        jax.ShapeDtypeStruct((B_HEADS, VT_ROWS, n), BF16),
    )
    return pl.pallas_call(
        functools.partial(_ffn_inproj_kernel, layer=layer),
        grid=(n // tm,),
        in_specs=[row(D_MODEL)] + _ffn_weight_specs()
                 + [_resident((1, D_MODEL)), _resident((D_MODEL, IN_COLS_DUP)), _resident((B_COLS, D_MODEL))],
        out_specs=(row(D_MODEL), row(A_Q_COLS), row(A_KV_DUP_COLS), row(A_KV_DUP_COLS),
                   headrow, headrow, headcol),
        out_shape=out_shape,
        scratch_shapes=_ffn_scratch(tm),
        compiler_params=pltpu.CompilerParams(
            dimension_semantics=("arbitrary",), vmem_limit_bytes=VMEM_LIMIT),
        name="ffn_inproj",
    )(x, g, wg, wu, wd, mg, win, wvt)


def _ffn_call(x, g, wg, wu, wd, fg, layer, final_norm):
    n = x.shape[0]
    tm = TM_FFN
    row = pl.BlockSpec((tm, D_MODEL), lambda i: (i, 0))
    return pl.pallas_call(
        functools.partial(_ffn_kernel, layer=layer, final_norm=final_norm),
        grid=(n // tm,),
        in_specs=[row] + _ffn_weight_specs() + [_resident((1, D_MODEL))],
        out_specs=row,
        out_shape=jax.ShapeDtypeStruct((n, D_MODEL), F32),
        scratch_shapes=_ffn_scratch(tm),
        compiler_params=pltpu.CompilerParams(
            dimension_semantics=("arbitrary",), vmem_limit_bytes=VMEM_LIMIT),
        name="ffn",
    )(x, g, wg, wu, wd, fg)


def _windowed_gqa(q0, qa_ref, ka_ref, va_ref, sink_ref, seq):
    tq = qa_ref.shape[0]
    mlo, mhi = _half_masks(BF16)
    lane = lax.broadcasted_iota(jnp.int32, (A_BLOCK, LANES), 1)
    a_idx = lax.broadcasted_iota(jnp.int32, (A_BLOCK, A_KEYS), 0)
    b_idx = lax.broadcasted_iota(jnp.int32, (A_BLOCK, A_KEYS), 1)
    row_blocks = []
    for r in range(tq // A_BLOCK):
        qs = q0 + r * A_BLOCK
        ws = pl.multiple_of(jnp.clip(qs - A_BLOCK, 0, seq - A_KEYS), A_BLOCK)
        dist = jnp.abs((qs - ws) + a_idx - b_idx)
        valid = dist <= WINDOW
        distf = dist.astype(F32)
        slabs = []
        for kv in range(A_KV_HEADS):
            kwin = ka_ref[pl.ds(ws, A_KEYS), kv * LANES:(kv + 1) * LANES]
            vwin = va_ref[pl.ds(ws, A_KEYS), kv * LANES:(kv + 1) * LANES]
            parts = []
            for g in range(A_GROUP):
                h = kv * A_GROUP + g
                slab = qa_ref[r * A_BLOCK:(r + 1) * A_BLOCK, (h // 2) * LANES:(h // 2 + 1) * LANES]
                parts.append(slab * (mlo if h % 2 == 0 else mhi))
            s = _dot_nt(jnp.concatenate(parts, axis=0), kwin)
            es, dens = [], []
            for g in range(A_GROUP):
                h = kv * A_GROUP + g
                sg = s[g * A_BLOCK:(g + 1) * A_BLOCK]
                sg = jnp.where(valid, sg - SLOPES_A[h] * distf, NEG_INF)
                sk = sink_ref[h]
                m = jnp.maximum(jnp.max(sg, axis=-1, keepdims=True), sk)
                e = jnp.exp(sg - m)
                dens.append(jnp.sum(e, axis=-1, keepdims=True) + jnp.exp(sk - m))
                es.append(e.astype(BF16))
            pv = _dot(jnp.concatenate(es, axis=0), vwin)
            outs = [pv[g * A_BLOCK:(g + 1) * A_BLOCK] / dens[g] for g in range(A_GROUP)]
            slabs.append(jnp.where(lane < HEAD_DIM, outs[0], outs[1]))
            slabs.append(jnp.where(lane < HEAD_DIM, outs[2], outs[3]))
        row_blocks.append(jnp.concatenate(slabs, axis=1))
    return jnp.concatenate(row_blocks, axis=0)


def _attn_kernel(x_ref, qa_ref, qb_ref, qaug_ref, ka_ref, va_ref, kb_ref, vbt_ref, ksgn_ref, kaug_ref, fdt_ref,
                 wout_ref, sink_ref, lq1_ref, lk1_ref, lq2_ref, lk2_ref, sublnt_ref,
                 xo_ref, s0_scr, s1_scr, s2_scr, p0_scr, p1_scr, *, lam_init):
    tq = x_ref.shape[0]
    seq = kb_ref.shape[1]
    q0 = pl.multiple_of(pl.program_id(1) * tq, tq)
    diag = pl.ds(q0, tq)

    lam = (jnp.exp(jnp.sum(lq1_ref[...] * lk1_ref[...], axis=-1, keepdims=True))
           - jnp.exp(jnp.sum(lq2_ref[...] * lk2_ref[...], axis=-1, keepdims=True)) + lam_init)

    kaug_d = kaug_ref[diag, :]
    mlo, mhi = _half_masks(BF16)
    s_bufs = (s0_scr, s1_scr, s2_scr)
    p_bufs = (p0_scr, p1_scr)

    oa = _windowed_gqa(q0, qa_ref, ka_ref, va_ref, sink_ref, seq)
    folded = oa[:, :LANES] + oa[:, LANES:2 * LANES] + oa[:, 2 * LANES:3 * LANES] + oa[:, 3 * LANES:]
    bits = lax.shift_right_logical(pltpu.bitcast(folded, jnp.uint32), jnp.uint32(16))
    anchor_zero = lax.shift_right_logical(bits, jnp.uint32(16)).astype(F32).astype(BF16)
    oa = oa.astype(BF16)

    def scores(h):
        s_scr = s_bufs[h % len(s_bufs)]
        q = qb_ref[h]
        if h == B_HEADS - 1:
            q = q + anchor_zero
        aq = qaug_ref[h]
        lhs_q = jnp.concatenate(
            [jnp.concatenate([q * mlo, q * mhi], axis=0), jnp.concatenate([aq, aq], axis=0)], axis=1)
        kcat = jnp.concatenate([kb_ref[h], ksgn_ref[...]], axis=1)
        st = _dot_nt(kcat, lhs_q)
        s_scr[...] = st
        m_far = jnp.max(st, axis=0, keepdims=True)
        kd = jnp.concatenate([kb_ref[h, diag, :], kaug_d], axis=1)
        sd = _dot_nt(kd, lhs_q) - SLOPES_B[h] * fdt_ref[...]
        s_scr[diag, :] = sd
        return jnp.maximum(m_far, jnp.max(sd, axis=0, keepdims=True))

    def softmax(h, m):
        s_scr, p_scr = s_bufs[h % len(s_bufs)], p_bufs[h % 2]
        for r in range(seq // EXP_ROWS):
            rows = slice(r * EXP_ROWS, (r + 1) * EXP_ROWS)
            p_scr[rows, :] = jnp.exp((s_scr[rows, :] - m).astype(BF16))

    def values(h):
        ot = _dot(vbt_ref[h], p_bufs[h % 2][...])
        ot = ot[:LANES, :] * (1.0 / ot[LANES:LANES + 1, :])
        o = ot[:, :tq] - lam * ot[:, tq:]
        ms = jnp.mean(o * o, axis=0, keepdims=True)
        y = o * lax.rsqrt(ms + RMS_EPS) * sublnt_ref[...] * (1.0 - lam_init)
        return y.T.astype(BF16)

    ob = [None] * B_HEADS
    m = [None] * B_HEADS
    for step in range(B_HEADS + 2):
        if step < B_HEADS:
            m[step] = scores(step)
        if 1 <= step <= B_HEADS:
            softmax(step - 1, m[step - 1])
        if step >= 2:
            ob[step - 2] = values(step - 2)

    mix = jnp.concatenate([oa] + ob, axis=1)
    xo_ref[...] = x_ref[...] + _dot(mix, wout_ref[...])


def _attn_call(x, qa, ka, va, qb, kb, vbt, qaug, ksgn, kaug, fdt, wout, sink,
               lq1, lk1, lq2, lk2, sublnt, batch, seq, lam_init):
    n = x.shape[0]
    tq = TQ_ATTN
    nq = seq // tq
    once = pl.Buffered(1)
    tile = lambda cols: pl.BlockSpec((tq, cols), lambda b, i: (b * nq + i, 0))
    per_batch = lambda cols: pl.BlockSpec((seq, cols), lambda b, i: (b, 0), pipeline_mode=once)
    head_tile = pl.BlockSpec((B_HEADS, tq, LANES), lambda b, i: (0, b * nq + i, 0))
    smem = pl.BlockSpec(memory_space=pltpu.SMEM)
    small = lambda cols: _resident((1, cols))
    return pl.pallas_call(
        functools.partial(_attn_kernel, lam_init=lam_init),
        grid=(batch, nq),
        in_specs=[tile(D_MODEL), tile(A_Q_COLS), head_tile,
                  pl.BlockSpec((B_HEADS, tq, LANES), lambda b, i: (0, i, 0)),
                  per_batch(A_KV_DUP_COLS), per_batch(A_KV_DUP_COLS),
                  pl.BlockSpec((B_HEADS, seq, LANES), lambda b, i: (0, b, 0), pipeline_mode=once),
                  pl.BlockSpec((B_HEADS, VT_ROWS, seq), lambda b, i: (0, 0, b), pipeline_mode=once),
                  pl.BlockSpec((None, seq, LANES), lambda b, i: (i, 0, 0)),
                  _resident((seq, LANES)), _resident((tq, 2 * tq)), _resident((MIX_WIDTH, D_MODEL)), smem,
                  small(HEAD_DIM), small(HEAD_DIM), small(HEAD_DIM), small(HEAD_DIM), _resident((LANES, tq))],
        out_specs=tile(D_MODEL),
        out_shape=jax.ShapeDtypeStruct((n, D_MODEL), F32),
        scratch_shapes=[pltpu.VMEM((seq, 2 * tq), F32), pltpu.VMEM((seq, 2 * tq), F32),
                        pltpu.VMEM((seq, 2 * tq), F32),
                        pltpu.VMEM((seq, 2 * tq), BF16), pltpu.VMEM((seq, 2 * tq), BF16)],
        compiler_params=pltpu.CompilerParams(
            dimension_semantics=("arbitrary", "arbitrary"), vmem_limit_bytes=VMEM_LIMIT),
        name="attn",
    )(x, qa, qb, qaug, ka, va, kb, vbt, ksgn, kaug, fdt, wout, sink, lq1, lk1, lq2, lk2, sublnt)


@functools.lru_cache(maxsize=None)
def _position_tables(seq, tq):
    f32, bf16 = np.float32, ml_dtypes.bfloat16
    pos = np.arange(seq, dtype=np.int32)
    lo = (pos % LANES).astype(f32)
    hi = (LANES * (pos // LANES)).astype(f32)
    one = np.ones((seq,), f32)
    pad = np.zeros((seq, LANES - 4), f32)
    kaug = np.concatenate([np.stack([lo, hi, one, one], axis=1), pad], axis=1)
    slopes = np.asarray(SLOPES_B, f32)[:, None, None]
    qcols = np.stack([one, one, -lo, -hi], axis=1)[None]
    assert DIAG_COL == 4
    qaug = np.concatenate([slopes * qcols, np.full((B_HEADS, seq, 1), DIAG_MASK, f32),
                           np.zeros((B_HEADS, seq, LANES - 5), f32)], axis=2)
    tile_of_key = (pos // tq)[None, :, None]
    tile = np.arange(seq // tq, dtype=np.int32)[:, None, None]
    flag = (np.arange(LANES) == DIAG_COL)[None, None, :] & (tile_of_key == tile)
    ksgn = np.where(tile_of_key <= tile, kaug[None], -kaug[None]) + flag.astype(f32)
    key = np.arange(tq, dtype=np.int32)[:, None]
    qry = np.arange(2 * tq, dtype=np.int32)[None, :] % tq
    fdt = (2 * np.maximum(key - qry, 0)).astype(f32)
    return qaug.astype(bf16), kaug.astype(bf16), ksgn.astype(bf16), fdt


def _prep_w_in(w):
    scale = HEAD_DIM ** -0.5
    c = 0
    qa = w[:, c:c + A_Q_COLS] * scale; c += A_Q_COLS
    ka = w[:, c:c + A_KV_HEADS * HEAD_DIM]; c += A_KV_HEADS * HEAD_DIM
    va = w[:, c:c + A_KV_HEADS * HEAD_DIM]; c += A_KV_HEADS * HEAD_DIM
    qb = w[:, c:c + B_COLS] * scale; c += B_COLS
    kb = w[:, c:c + B_COLS]; c += B_COLS
    vb = w[:, c:c + B_COLS]
    dup = lambda t: jnp.concatenate(
        [t[:, j * HEAD_DIM:(j + 1) * HEAD_DIM] for j in range(A_KV_HEADS) for _ in range(2)], axis=1)
    return jnp.concatenate([qa, dup(ka), dup(va), qb, kb], axis=1).astype(BF16), vb.T.astype(BF16)


def kernel(x, ffn1_norm, ffn1_w_gate, ffn1_w_up, ffn1_w_down, mix_norm, w_in, sink,
           lam_q1, lam_k1, lam_q2, lam_k2, diff_subln, w_out,
           ffn2_norm, ffn2_w_gate, ffn2_w_up, ffn2_w_down, final_norm):
    batch, seq, d = x.shape
    depth = w_in.shape[0]
    assert d == D_MODEL and seq % TQ_ATTN == 0 and (batch * seq) % TM_FFN == 0
    xf = x.reshape(batch * seq, d)
    qaug, kaug, ksgn, fdt = _position_tables(seq, TQ_ATTN)
    row = lambda v: v.reshape(1, -1).astype(F32)
    for l in range(depth):
        lam_init = 0.8 - 0.6 * math.exp(-0.3 * l)
        win, wvt = _prep_w_in(w_in[l])
        xf, qa, ka, va, qb, kb, vbt = _ffn_inproj_call(
            xf, row(ffn1_norm[l]), ffn1_w_gate, ffn1_w_up, ffn1_w_down, row(mix_norm[l]), win, wvt, l)
        sublnt = jnp.broadcast_to(diff_subln[l].astype(F32)[:, None], (LANES, TQ_ATTN))
        xf = _attn_call(xf, qa, ka, va, qb, kb, vbt, qaug, ksgn, kaug, fdt, w_out[l].astype(BF16),
                        sink[l].astype(F32), row(lam_q1[l]), row(lam_k1[l]),
                        row(lam_q2[l]), row(lam_k2[l]), sublnt, batch, seq, lam_init)
        xf = _ffn_call(xf, row(ffn2_norm[l]), ffn2_w_gate, ffn2_w_up, ffn2_w_down, row(final_norm),
                       l, final_norm=(l == depth - 1))
    return xf.reshape(batch, seq, d)
```

```python
import functools
import math

import jax
import jax.numpy as jnp
import ml_dtypes
import numpy as np
from jax import lax
from jax.experimental import pallas as pl
from jax.experimental.pallas import tpu as pltpu

F32 = jnp.float32
BF16 = jnp.bfloat16

D_MODEL = 1024
D_FF = 2816
HEAD_DIM = 64
WINDOW = 128
A_Q_HEADS = 8
A_KV_HEADS = 2
A_GROUP = A_Q_HEADS // A_KV_HEADS
B_HEADS = 4
RMS_EPS = 1e-6
NEG_INF = -1e30

LANES = 128
MXU_DIM = 256
BF16_ROWS = 16
VT_ROWS = LANES + BF16_ROWS
A_Q_COLS = A_Q_HEADS * HEAD_DIM
B_COLS = B_HEADS * LANES
A_KV_DUP_COLS = 2 * A_KV_HEADS * HEAD_DIM
IN_COLS_DUP = A_Q_COLS + A_KV_DUP_COLS + 2 * B_COLS
VT_COLS = B_COLS + A_KV_HEADS * HEAD_DIM
VAT_ROWS = HEAD_DIM + BF16_ROWS
MIX_WIDTH = A_Q_COLS + B_COLS

FF_CHUNK = MXU_DIM
TM_FFN = 512
W_STAGE_CHUNKS = 8
W_STAGE_SLOTS = 3
TQ_ATTN = 256
A_BLOCK = 128
A_KEYS = 3 * A_BLOCK
EXP_ROWS = 64
DIAG_COL = 4
DIAG_MASK = -32768.0
VMEM_LIMIT = 56 * 1024 * 1024

SLOPES_A = tuple(2.0 ** (-8.0 * i / A_Q_HEADS) for i in range(1, A_Q_HEADS + 1))
SLOPES_B = tuple(2.0 ** (-8.0 * i / B_HEADS) for i in range(1, B_HEADS + 1))


def _rms(x, g):
    ms = jnp.mean(x * x, axis=-1, keepdims=True)
    return x * lax.rsqrt(ms + RMS_EPS) * g


def _dot(a, b):
    return jnp.dot(a, b, preferred_element_type=F32)


def _dot_nt(a, b):
    return lax.dot_general(a, b, (((1,), (1,)), ((), ())), preferred_element_type=F32)


def _half_masks(dtype):
    lane = lax.broadcasted_iota(jnp.int32, (1, LANES), 1)
    lo = (lane < HEAD_DIM).astype(F32).astype(dtype)
    hi = (lane >= HEAD_DIM).astype(F32).astype(dtype)
    return lo, hi


def _stage_bf16(w_hbm, layer, w_bf, stage, sem):
    slots, rows = stage.shape[0], stage.shape[1]
    n_chunks = w_bf.shape[0] // rows

    def copy(c):
        return pltpu.make_async_copy(
            w_hbm.at[layer, pl.ds(c * rows, rows), :], stage.at[c % slots], sem.at[c % slots])

    for c in range(min(slots - 1, n_chunks)):
        copy(c).start()
    for c in range(n_chunks):
        ahead = c + slots - 1
        if ahead < n_chunks:
            copy(ahead).start()
        copy(c).wait()
        w_bf[c * rows:(c + 1) * rows, :] = stage[c % slots].astype(BF16)


def _stage_ffn_weights(layer, wg_hbm, wu_hbm, wd_hbm, wg_bf, wu_bf, wd_bf, stage_up, stage_dn, sem):
    @pl.when(pl.program_id(0) == 0)
    def _():
        _stage_bf16(wg_hbm, layer, wg_bf, stage_up, sem)
        _stage_bf16(wu_hbm, layer, wu_bf, stage_up, sem)
        _stage_bf16(wd_hbm, layer, wd_bf, stage_dn, sem)


def _swiglu_residual(x, g_ref, wg_ref, wu_ref, wd_ref, h_scr):
    xn = _rms(x, g_ref[...]).astype(BF16)
    for c in range(D_FF // FF_CHUNK):
        sl = slice(c * FF_CHUNK, (c + 1) * FF_CHUNK)
        gate = _dot(xn, wg_ref[:, sl])
        up = _dot(xn, wu_ref[:, sl])
        h_scr[:, sl] = (gate * jax.nn.sigmoid(gate) * up).astype(BF16)
    return x + 0.5 * _dot(h_scr[...], wd_ref[...])


def _ffn_inproj_kernel(x_ref, g_ref, wg_hbm, wu_hbm, wd_hbm, mg_ref, win_ref, wvt_ref,
                       xo_ref, qa_ref, ka_ref, vat_ref, qb_ref, kb_ref, vbt_ref,
                       h_scr, wg_ref, wu_ref, wd_ref, stage_up, stage_dn, sem, *, layer):
    _stage_ffn_weights(layer, wg_hbm, wu_hbm, wd_hbm, wg_ref, wu_ref, wd_ref, stage_up, stage_dn, sem)
    x1 = _swiglu_residual(x_ref[...], g_ref, wg_ref, wu_ref, wd_ref, h_scr)
    xo_ref[...] = x1
    hn = _rms(x1, mg_ref[...]).astype(BF16)
    proj = _dot(hn, win_ref[...]).astype(BF16)
    c0 = A_Q_COLS
    c2 = c0 + A_KV_DUP_COLS
    qa_ref[...] = proj[:, :c0]
    ka_ref[...] = proj[:, c0:c2]
    vt = _dot_nt(wvt_ref[...], hn).astype(BF16)
    ones = jnp.ones((BF16_ROWS, vt.shape[1]), BF16)
    for kv in range(A_KV_HEADS):
        vat_ref[kv, :HEAD_DIM, :] = vt[B_COLS + kv * HEAD_DIM:B_COLS + (kv + 1) * HEAD_DIM, :]
        vat_ref[kv, HEAD_DIM:, :] = ones
    for h in range(B_HEADS):
        qb_ref[h] = proj[:, c2 + h * LANES: c2 + (h + 1) * LANES]
        kb_ref[h] = proj[:, c2 + B_COLS + h * LANES: c2 + B_COLS + (h + 1) * LANES]
        vbt_ref[h, :LANES, :] = vt[h * LANES:(h + 1) * LANES, :]
        vbt_ref[h, LANES:, :] = ones


def _ffn_kernel(x_ref, g_ref, wg_hbm, wu_hbm, wd_hbm, fg_ref, xo_ref,
                h_scr, wg_ref, wu_ref, wd_ref, stage_up, stage_dn, sem, *, layer, final_norm):
    _stage_ffn_weights(layer, wg_hbm, wu_hbm, wd_hbm, wg_ref, wu_ref, wd_ref, stage_up, stage_dn, sem)
    x1 = _swiglu_residual(x_ref[...], g_ref, wg_ref, wu_ref, wd_ref, h_scr)
    if final_norm:
        x1 = _rms(x1, fg_ref[...])
    xo_ref[...] = x1


def _resident(shape):
    nd = len(shape)
    return pl.BlockSpec(shape, lambda *_: (0,) * nd, pipeline_mode=pl.Buffered(1))


def _ffn_weight_specs():
    hbm = pl.BlockSpec(memory_space=pl.ANY)
    return [_resident((1, D_MODEL)), hbm, hbm, hbm]


def _ffn_scratch(tm):
    up_rows = D_MODEL // W_STAGE_CHUNKS
    dn_rows = D_FF // W_STAGE_CHUNKS
    return [pltpu.VMEM((tm, D_FF), BF16),
            pltpu.VMEM((D_MODEL, D_FF), BF16), pltpu.VMEM((D_MODEL, D_FF), BF16), pltpu.VMEM((D_FF, D_MODEL), BF16),
            pltpu.VMEM((W_STAGE_SLOTS, up_rows, D_FF), F32), pltpu.VMEM((W_STAGE_SLOTS, dn_rows, D_MODEL), F32),
            pltpu.SemaphoreType.DMA((W_STAGE_SLOTS,))]


def _ffn_inproj_call(x, g, wg, wu, wd, mg, win, wvt, layer):
    n = x.shape[0]
    tm = TM_FFN
    row = lambda cols: pl.BlockSpec((tm, cols), lambda i: (i, 0))
    headrow = pl.BlockSpec((B_HEADS, tm, LANES), lambda i: (0, i, 0))
    headcol = pl.BlockSpec((B_HEADS, VT_ROWS, tm), lambda i: (0, 0, i))
    kvcol = pl.BlockSpec((A_KV_HEADS, VAT_ROWS, tm), lambda i: (0, 0, i))
    out_shape = (
        jax.ShapeDtypeStruct((n, D_MODEL), F32),
        jax.ShapeDtypeStruct((n, A_Q_COLS), BF16),
        jax.ShapeDtypeStruct((n, A_KV_DUP_COLS), BF16),
        jax.ShapeDtypeStruct((A_KV_HEADS, VAT_ROWS, n), BF16),
        jax.ShapeDtypeStruct((B_HEADS, n, LANES), BF16),
        jax.ShapeDtypeStruct((B_HEADS, n, LANES), BF16),
        jax.ShapeDtypeStruct((B_HEADS, VT_ROWS, n), BF16),
    )
    return pl.pallas_call(
        functools.partial(_ffn_inproj_kernel, layer=layer),
        grid=(n // tm,),
        in_specs=[row(D_MODEL)] + _ffn_weight_specs()
                 + [_resident((1, D_MODEL)), _resident((D_MODEL, IN_COLS_DUP)), _resident((VT_COLS, D_MODEL))],
        out_specs=(row(D_MODEL), row(A_Q_COLS), row(A_KV_DUP_COLS), kvcol,
                   headrow, headrow, headcol),
        out_shape=out_shape,
        scratch_shapes=_ffn_scratch(tm),
        compiler_params=pltpu.CompilerParams(
            dimension_semantics=("arbitrary",), vmem_limit_bytes=VMEM_LIMIT),
        name="ffn_inproj",
    )(x, g, wg, wu, wd, mg, win, wvt)


def _ffn_call(x, g, wg, wu, wd, fg, layer, final_norm):
    n = x.shape[0]
    tm = TM_FFN
    row = pl.BlockSpec((tm, D_MODEL), lambda i: (i, 0))
    return pl.pallas_call(
        functools.partial(_ffn_kernel, layer=layer, final_norm=final_norm),
        grid=(n // tm,),
        in_specs=[row] + _ffn_weight_specs() + [_resident((1, D_MODEL))],
        out_specs=row,
        out_shape=jax.ShapeDtypeStruct((n, D_MODEL), F32),
        scratch_shapes=_ffn_scratch(tm),
        compiler_params=pltpu.CompilerParams(
            dimension_semantics=("arbitrary",), vmem_limit_bytes=VMEM_LIMIT),
        name="ffn",
    )(x, g, wg, wu, wd, fg)


def _windowed_gqa(q0, qa_ref, ka_ref, vat_ref, sink_ref, seq):
    tq = qa_ref.shape[0]
    mlo, mhi = _half_masks(BF16)
    b_idx = lax.broadcasted_iota(jnp.int32, (A_KEYS, A_BLOCK), 0)
    a_idx = lax.broadcasted_iota(jnp.int32, (A_KEYS, A_BLOCK), 1)
    row_blocks = []
    for r in range(tq // A_BLOCK):
        qs = q0 + r * A_BLOCK
        ws = pl.multiple_of(jnp.clip(qs - A_BLOCK, 0, seq - A_KEYS), A_BLOCK)
        dist = jnp.abs((qs - ws) + a_idx - b_idx)
        valid = dist <= WINDOW
        distf = dist.astype(F32)
        slabs = []
        for kv in range(A_KV_HEADS):
            kwin = ka_ref[pl.ds(ws, A_KEYS), kv * LANES:(kv + 1) * LANES]
            vwin = vat_ref[kv, :, pl.ds(ws, A_KEYS)]
            parts = []
            for g in range(A_GROUP):
                h = kv * A_GROUP + g
                slab = qa_ref[r * A_BLOCK:(r + 1) * A_BLOCK, (h // 2) * LANES:(h // 2 + 1) * LANES]
                parts.append(slab * (mlo if h % 2 == 0 else mhi))
            st = _dot_nt(kwin, jnp.concatenate(parts, axis=0))
            es, sinks = [], []
            for g in range(A_GROUP):
                h = kv * A_GROUP + g
                sg = st[:, g * A_BLOCK:(g + 1) * A_BLOCK]
                sg = jnp.where(valid, sg - SLOPES_A[h] * distf, NEG_INF)
                sk = sink_ref[h]
                m = jnp.maximum(jnp.max(sg, axis=0, keepdims=True), sk)
                es.append(jnp.exp((sg - m).astype(BF16)))
                sinks.append(jnp.exp(sk - m))
            ot = _dot(vwin, jnp.concatenate(es, axis=1))
            den = ot[HEAD_DIM:HEAD_DIM + 1, :] + jnp.concatenate(sinks, axis=1)
            o = ot[:HEAD_DIM, :] * (1.0 / den)
            for j in range(A_GROUP // 2):
                pair = jnp.concatenate([o[:, (2 * j) * A_BLOCK:(2 * j + 1) * A_BLOCK],
                                        o[:, (2 * j + 1) * A_BLOCK:(2 * j + 2) * A_BLOCK]], axis=0)
                slabs.append(pair.T)
        row_blocks.append(jnp.concatenate(slabs, axis=1))
    return jnp.concatenate(row_blocks, axis=0)


def _attn_kernel(x_ref, qa_ref, qb_ref, qaug_ref, ka_ref, vat_ref, kb_ref, vbt_ref, ksgn_ref, kaug_ref, fdt_ref,
                 wout_ref, sink_ref, lq1_ref, lk1_ref, lq2_ref, lk2_ref, sublnt_ref,
                 xo_ref, s0_scr, s1_scr, s2_scr, p0_scr, p1_scr, *, lam_init):
    tq = x_ref.shape[0]
    seq = kb_ref.shape[1]
    q0 = pl.multiple_of(pl.program_id(1) * tq, tq)
    diag = pl.ds(q0, tq)

    lam = (jnp.exp(jnp.sum(lq1_ref[...] * lk1_ref[...], axis=-1, keepdims=True))
           - jnp.exp(jnp.sum(lq2_ref[...] * lk2_ref[...], axis=-1, keepdims=True)) + lam_init)

    kaug_d = kaug_ref[diag, :]
    mlo, mhi = _half_masks(BF16)
    s_bufs = (s0_scr, s1_scr, s2_scr)
    p_bufs = (p0_scr, p1_scr)

    oa = _windowed_gqa(q0, qa_ref, ka_ref, vat_ref, sink_ref, seq)
    folded = oa[:, :LANES] + oa[:, LANES:2 * LANES] + oa[:, 2 * LANES:3 * LANES] + oa[:, 3 * LANES:]
    bits = lax.shift_right_logical(pltpu.bitcast(folded, jnp.uint32), jnp.uint32(16))
    anchor_zero = lax.shift_right_logical(bits, jnp.uint32(16)).astype(F32).astype(BF16)
    oa = oa.astype(BF16)

    def scores(h):
        s_scr = s_bufs[h % len(s_bufs)]
        q = qb_ref[h]
        if h == B_HEADS - 1:
            q = q + anchor_zero
        aq = qaug_ref[h]
        lhs_q = jnp.concatenate(
            [jnp.concatenate([q * mlo, q * mhi], axis=0), jnp.concatenate([aq, aq], axis=0)], axis=1)
        kcat = jnp.concatenate([kb_ref[h], ksgn_ref[...]], axis=1)
        st = _dot_nt(kcat, lhs_q)
        s_scr[...] = st
        m_far = jnp.max(st, axis=0, keepdims=True)
        kd = jnp.concatenate([kb_ref[h, diag, :], kaug_d], axis=1)
        sd = _dot_nt(kd, lhs_q) - SLOPES_B[h] * fdt_ref[...]
        s_scr[diag, :] = sd
        return jnp.maximum(m_far, jnp.max(sd, axis=0, keepdims=True))

    def softmax(h, m):
        s_scr, p_scr = s_bufs[h % len(s_bufs)], p_bufs[h % 2]
        for r in range(seq // EXP_ROWS):
            rows = slice(r * EXP_ROWS, (r + 1) * EXP_ROWS)
            p_scr[rows, :] = jnp.exp((s_scr[rows, :] - m).astype(BF16))

    def values(h):
        ot = _dot(vbt_ref[h], p_bufs[h % 2][...])
        ot = ot[:LANES, :] * (1.0 / ot[LANES:LANES + 1, :])
        o = ot[:, :tq] - lam * ot[:, tq:]
        ms = jnp.mean(o * o, axis=0, keepdims=True)
        y = o * lax.rsqrt(ms + RMS_EPS) * sublnt_ref[...] * (1.0 - lam_init)
        return y.T.astype(BF16)

    ob = [None] * B_HEADS
    m = [None] * B_HEADS
    for step in range(B_HEADS + 2):
        if step < B_HEADS:
            m[step] = scores(step)
        if 1 <= step <= B_HEADS:
            softmax(step - 1, m[step - 1])
        if step >= 2:
            ob[step - 2] = values(step - 2)

    mix = jnp.concatenate([oa] + ob, axis=1)
    xo_ref[...] = x_ref[...] + _dot(mix, wout_ref[...])


def _attn_call(x, qa, ka, vat, qb, kb, vbt, qaug, ksgn, kaug, fdt, wout, sink,
               lq1, lk1, lq2, lk2, sublnt, batch, seq, lam_init):
    n = x.shape[0]
    tq = TQ_ATTN
    nq = seq // tq
    once = pl.Buffered(1)
    tile = lambda cols: pl.BlockSpec((tq, cols), lambda b, i: (b * nq + i, 0))
    per_batch = lambda cols: pl.BlockSpec((seq, cols), lambda b, i: (b, 0), pipeline_mode=once)
    head_tile = pl.BlockSpec((B_HEADS, tq, LANES), lambda b, i: (0, b * nq + i, 0))
    smem = pl.BlockSpec(memory_space=pltpu.SMEM)
    small = lambda cols: _resident((1, cols))
    return pl.pallas_call(
        functools.partial(_attn_kernel, lam_init=lam_init),
        grid=(batch, nq),
        in_specs=[tile(D_MODEL), tile(A_Q_COLS), head_tile,
                  pl.BlockSpec((B_HEADS, tq, LANES), lambda b, i: (0, i, 0)),
                  per_batch(A_KV_DUP_COLS),
                  pl.BlockSpec((A_KV_HEADS, VAT_ROWS, seq), lambda b, i: (0, 0, b), pipeline_mode=once),
                  pl.BlockSpec((B_HEADS, seq, LANES), lambda b, i: (0, b, 0), pipeline_mode=once),
                  pl.BlockSpec((B_HEADS, VT_ROWS, seq), lambda b, i: (0, 0, b), pipeline_mode=once),
                  pl.BlockSpec((None, seq, LANES), lambda b, i: (i, 0, 0)),
                  _resident((seq, LANES)), _resident((tq, 2 * tq)), _resident((MIX_WIDTH, D_MODEL)), smem,
                  small(HEAD_DIM), small(HEAD_DIM), small(HEAD_DIM), small(HEAD_DIM), _resident((LANES, tq))],
        out_specs=tile(D_MODEL),
        out_shape=jax.ShapeDtypeStruct((n, D_MODEL), F32),
        scratch_shapes=[pltpu.VMEM((seq, 2 * tq), F32), pltpu.VMEM((seq, 2 * tq), F32),
                        pltpu.VMEM((seq, 2 * tq), F32),
                        pltpu.VMEM((seq, 2 * tq), BF16), pltpu.VMEM((seq, 2 * tq), BF16)],
        compiler_params=pltpu.CompilerParams(
            dimension_semantics=("arbitrary", "arbitrary"), vmem_limit_bytes=VMEM_LIMIT),
        name="attn",
    )(x, qa, qb, qaug, ka, vat, kb, vbt, ksgn, kaug, fdt, wout, sink, lq1, lk1, lq2, lk2, sublnt)


@functools.lru_cache(maxsize=None)
def _position_tables(seq, tq):
    f32, bf16 = np.float32, ml_dtypes.bfloat16
    pos = np.arange(seq, dtype=np.int32)
    lo = (pos % LANES).astype(f32)
    hi = (LANES * (pos // LANES)).astype(f32)
    one = np.ones((seq,), f32)
    pad = np.zeros((seq, LANES - 4), f32)
    kaug = np.concatenate([np.stack([lo, hi, one, one], axis=1), pad], axis=1)
    slopes = np.asarray(SLOPES_B, f32)[:, None, None]
    qcols = np.stack([one, one, -lo, -hi], axis=1)[None]
    assert DIAG_COL == 4
    qaug = np.concatenate([slopes * qcols, np.full((B_HEADS, seq, 1), DIAG_MASK, f32),
                           np.zeros((B_HEADS, seq, LANES - 5), f32)], axis=2)
    tile_of_key = (pos // tq)[None, :, None]
    tile = np.arange(seq // tq, dtype=np.int32)[:, None, None]
    flag = (np.arange(LANES) == DIAG_COL)[None, None, :] & (tile_of_key == tile)
    ksgn = np.where(tile_of_key <= tile, kaug[None], -kaug[None]) + flag.astype(f32)
    key = np.arange(tq, dtype=np.int32)[:, None]
    qry = np.arange(2 * tq, dtype=np.int32)[None, :] % tq
    fdt = (2 * np.maximum(key - qry, 0)).astype(f32)
    return qaug.astype(bf16), kaug.astype(bf16), ksgn.astype(bf16), fdt


def _prep_w_in(w):
    scale = HEAD_DIM ** -0.5
    c = 0
    qa = w[:, c:c + A_Q_COLS] * scale; c += A_Q_COLS
    ka = w[:, c:c + A_KV_HEADS * HEAD_DIM]; c += A_KV_HEADS * HEAD_DIM
    va = w[:, c:c + A_KV_HEADS * HEAD_DIM]; c += A_KV_HEADS * HEAD_DIM
    qb = w[:, c:c + B_COLS] * scale; c += B_COLS
    kb = w[:, c:c + B_COLS]; c += B_COLS
    vb = w[:, c:c + B_COLS]
    dup = lambda t: jnp.concatenate(
        [t[:, j * HEAD_DIM:(j + 1) * HEAD_DIM] for j in range(A_KV_HEADS) for _ in range(2)], axis=1)
    wvt = jnp.concatenate([vb, va], axis=1).T
    return jnp.concatenate([qa, dup(ka), qb, kb], axis=1).astype(BF16), wvt.astype(BF16)


def kernel(x, ffn1_norm, ffn1_w_gate, ffn1_w_up, ffn1_w_down, mix_norm, w_in, sink,
           lam_q1, lam_k1, lam_q2, lam_k2, diff_subln, w_out,
           ffn2_norm, ffn2_w_gate, ffn2_w_up, ffn2_w_down, final_norm):
    batch, seq, d = x.shape
    depth = w_in.shape[0]
    assert d == D_MODEL and seq % TQ_ATTN == 0 and (batch * seq) % TM_FFN == 0
    xf = x.reshape(batch * seq, d)
    qaug, kaug, ksgn, fdt = _position_tables(seq, TQ_ATTN)
    row = lambda v: v.reshape(1, -1).astype(F32)
    for l in range(depth):
        lam_init = 0.8 - 0.6 * math.exp(-0.3 * l)
        win, wvt = _prep_w_in(w_in[l])
        xf, qa, ka, vat, qb, kb, vbt = _ffn_inproj_call(
            xf, row(ffn1_norm[l]), ffn1_w_gate, ffn1_w_up, ffn1_w_down, row(mix_norm[l]), win, wvt, l)
        sublnt = jnp.broadcast_to(diff_subln[l].astype(F32)[:, None], (LANES, TQ_ATTN))
        xf = _attn_call(xf, qa, ka, vat, qb, kb, vbt, qaug, ksgn, kaug, fdt, w_out[l].astype(BF16),
                        sink[l].astype(F32), row(lam_q1[l]), row(lam_k1[l]),
                        row(lam_q2[l]), row(lam_k2[l]), sublnt, batch, seq, lam_init)
        xf = _ffn_call(xf, row(ffn2_norm[l]), ffn2_w_gate, ffn2_w_up, ffn2_w_down, row(final_norm),
                       l, final_norm=(l == depth - 1))
    return xf.reshape(batch, seq, d)
```

```python
import functools
import math

import jax
import jax.numpy as jnp
import ml_dtypes
import numpy as np
from jax import lax
from jax.experimental import pallas as pl
from jax.experimental.pallas import tpu as pltpu

F32 = jnp.float32
BF16 = jnp.bfloat16

D_MODEL = 1024
D_FF = 2816
HEAD_DIM = 64
WINDOW = 128
A_Q_HEADS = 8
A_KV_HEADS = 2
A_GROUP = A_Q_HEADS // A_KV_HEADS
B_HEADS = 4
RMS_EPS = 1e-6
NEG_INF = -1e30

LANES = 128
MXU_DIM = 256
BF16_ROWS = 16
VT_ROWS = LANES + BF16_ROWS
A_Q_COLS = A_Q_HEADS * HEAD_DIM
B_COLS = B_HEADS * LANES
A_KV_DUP_COLS = 2 * A_KV_HEADS * HEAD_DIM
IN_COLS_DUP = A_Q_COLS + A_KV_DUP_COLS + 2 * B_COLS
VT_COLS = B_COLS + A_KV_HEADS * HEAD_DIM
VAT_ROWS = HEAD_DIM + BF16_ROWS
MIX_WIDTH = A_Q_COLS + B_COLS

FF_CHUNK = MXU_DIM
TM_FFN = 512
W_STAGE_CHUNKS = 8
W_STAGE_SLOTS = 3
TQ_ATTN = 256
A_BLOCK = 128
A_KEYS = 3 * A_BLOCK
EXP_ROWS = 64
DIAG_COL = 4
DIAG_MASK = -(2.0 ** 100)
VMEM_LIMIT = 56 * 1024 * 1024

SLOPES_A = tuple(2.0 ** (-8.0 * i / A_Q_HEADS) for i in range(1, A_Q_HEADS + 1))
SLOPES_B = tuple(2.0 ** (-8.0 * i / B_HEADS) for i in range(1, B_HEADS + 1))


def _rms(x, g):
    ms = jnp.mean(x * x, axis=-1, keepdims=True)
    return x * lax.rsqrt(ms + RMS_EPS) * g


def _dot(a, b):
    return jnp.dot(a, b, preferred_element_type=F32)


def _dot_nt(a, b):
    return lax.dot_general(a, b, (((1,), (1,)), ((), ())), preferred_element_type=F32)


def _half_masks(dtype):
    lane = lax.broadcasted_iota(jnp.int32, (1, LANES), 1)
    lo = (lane < HEAD_DIM).astype(F32).astype(dtype)
    hi = (lane >= HEAD_DIM).astype(F32).astype(dtype)
    return lo, hi


def _stage_bf16(w_hbm, layer, w_bf, stage, sem):
    slots, rows = stage.shape[0], stage.shape[1]
    n_chunks = w_bf.shape[0] // rows

    def copy(c):
        return pltpu.make_async_copy(
            w_hbm.at[layer, pl.ds(c * rows, rows), :], stage.at[c % slots], sem.at[c % slots])

    for c in range(min(slots - 1, n_chunks)):
        copy(c).start()
    for c in range(n_chunks):
        ahead = c + slots - 1
        if ahead < n_chunks:
            copy(ahead).start()
        copy(c).wait()
        w_bf[c * rows:(c + 1) * rows, :] = stage[c % slots].astype(BF16)


def _stage_ffn_weights(layer, wg_hbm, wu_hbm, wd_hbm, wg_bf, wu_bf, wd_bf, stage_up, stage_dn, sem):
    @pl.when(pl.program_id(0) == 0)
    def _():
        _stage_bf16(wg_hbm, layer, wg_bf, stage_up, sem)
        _stage_bf16(wu_hbm, layer, wu_bf, stage_up, sem)
        _stage_bf16(wd_hbm, layer, wd_bf, stage_dn, sem)


def _swiglu_residual(x, g_ref, wg_ref, wu_ref, wd_ref, h_scr):
    xn = _rms(x, g_ref[...]).astype(BF16)
    for c in range(D_FF // FF_CHUNK):
        sl = slice(c * FF_CHUNK, (c + 1) * FF_CHUNK)
        gate = _dot(xn, wg_ref[:, sl])
        up = _dot(xn, wu_ref[:, sl])
        h_scr[:, sl] = (gate * jax.nn.sigmoid(gate) * up).astype(BF16)
    return x + 0.5 * _dot(h_scr[...], wd_ref[...])


def _ffn_inproj_kernel(x_ref, g_ref, wg_hbm, wu_hbm, wd_hbm, mg_ref, win_ref, wvt_ref,
                       xo_ref, qa_ref, ka_ref, vat_ref, qb_ref, kb_ref, vbt_ref,
                       h_scr, wg_ref, wu_ref, wd_ref, stage_up, stage_dn, sem, *, layer):
    _stage_ffn_weights(layer, wg_hbm, wu_hbm, wd_hbm, wg_ref, wu_ref, wd_ref, stage_up, stage_dn, sem)
    x1 = _swiglu_residual(x_ref[...], g_ref, wg_ref, wu_ref, wd_ref, h_scr)
    xo_ref[...] = x1
    hn = _rms(x1, mg_ref[...]).astype(BF16)
    proj = _dot(hn, win_ref[...]).astype(BF16)
    c0 = A_Q_COLS
    c2 = c0 + A_KV_DUP_COLS
    qa_ref[...] = proj[:, :c0]
    ka_ref[...] = proj[:, c0:c2]
    vt = _dot_nt(wvt_ref[...], hn).astype(BF16)
    ones = jnp.ones((BF16_ROWS, vt.shape[1]), BF16)
    for kv in range(A_KV_HEADS):
        vat_ref[kv, :HEAD_DIM, :] = vt[B_COLS + kv * HEAD_DIM:B_COLS + (kv + 1) * HEAD_DIM, :]
        vat_ref[kv, HEAD_DIM:, :] = ones
    for h in range(B_HEADS):
        qb_ref[h] = proj[:, c2 + h * LANES: c2 + (h + 1) * LANES]
        kb_ref[h] = proj[:, c2 + B_COLS + h * LANES: c2 + B_COLS + (h + 1) * LANES]
        vbt_ref[h, :LANES, :] = vt[h * LANES:(h + 1) * LANES, :]
        vbt_ref[h, LANES:, :] = ones


def _ffn_kernel(x_ref, g_ref, wg_hbm, wu_hbm, wd_hbm, fg_ref, xo_ref,
                h_scr, wg_ref, wu_ref, wd_ref, stage_up, stage_dn, sem, *, layer, final_norm):
    _stage_ffn_weights(layer, wg_hbm, wu_hbm, wd_hbm, wg_ref, wu_ref, wd_ref, stage_up, stage_dn, sem)
    x1 = _swiglu_residual(x_ref[...], g_ref, wg_ref, wu_ref, wd_ref, h_scr)
    if final_norm:
        x1 = _rms(x1, fg_ref[...])
    xo_ref[...] = x1


def _resident(shape):
    nd = len(shape)
    return pl.BlockSpec(shape, lambda *_: (0,) * nd, pipeline_mode=pl.Buffered(1))


def _ffn_weight_specs():
    hbm = pl.BlockSpec(memory_space=pl.ANY)
    return [_resident((1, D_MODEL)), hbm, hbm, hbm]


def _ffn_scratch(tm):
    up_rows = D_MODEL // W_STAGE_CHUNKS
    dn_rows = D_FF // W_STAGE_CHUNKS
    return [pltpu.VMEM((tm, D_FF), BF16),
            pltpu.VMEM((D_MODEL, D_FF), BF16), pltpu.VMEM((D_MODEL, D_FF), BF16), pltpu.VMEM((D_FF, D_MODEL), BF16),
            pltpu.VMEM((W_STAGE_SLOTS, up_rows, D_FF), F32), pltpu.VMEM((W_STAGE_SLOTS, dn_rows, D_MODEL), F32),
            pltpu.SemaphoreType.DMA((W_STAGE_SLOTS,))]


def _ffn_inproj_call(x, g, wg, wu, wd, mg, win, wvt, layer):
    n = x.shape[0]
    tm = TM_FFN
    row = lambda cols: pl.BlockSpec((tm, cols), lambda i: (i, 0))
    headrow = pl.BlockSpec((B_HEADS, tm, LANES), lambda i: (0, i, 0))
    headcol = pl.BlockSpec((B_HEADS, VT_ROWS, tm), lambda i: (0, 0, i))
    kvcol = pl.BlockSpec((A_KV_HEADS, VAT_ROWS, tm), lambda i: (0, 0, i))
    out_shape = (
        jax.ShapeDtypeStruct((n, D_MODEL), F32),
        jax.ShapeDtypeStruct((n, A_Q_COLS), BF16),
        jax.ShapeDtypeStruct((n, A_KV_DUP_COLS), BF16),
        jax.ShapeDtypeStruct((A_KV_HEADS, VAT_ROWS, n), BF16),
        jax.ShapeDtypeStruct((B_HEADS, n, LANES), BF16),
        jax.ShapeDtypeStruct((B_HEADS, n, LANES), BF16),
        jax.ShapeDtypeStruct((B_HEADS, VT_ROWS, n), BF16),
    )
    return pl.pallas_call(
        functools.partial(_ffn_inproj_kernel, layer=layer),
        grid=(n // tm,),
        in_specs=[row(D_MODEL)] + _ffn_weight_specs()
                 + [_resident((1, D_MODEL)), _resident((D_MODEL, IN_COLS_DUP)), _resident((VT_COLS, D_MODEL))],
        out_specs=(row(D_MODEL), row(A_Q_COLS), row(A_KV_DUP_COLS), kvcol,
                   headrow, headrow, headcol),
        out_shape=out_shape,
        scratch_shapes=_ffn_scratch(tm),
        compiler_params=pltpu.CompilerParams(
            dimension_semantics=("arbitrary",), vmem_limit_bytes=VMEM_LIMIT),
        name="ffn_inproj",
    )(x, g, wg, wu, wd, mg, win, wvt)


def _ffn_call(x, g, wg, wu, wd, fg, layer, final_norm):
    n = x.shape[0]
    tm = TM_FFN
    row = pl.BlockSpec((tm, D_MODEL), lambda i: (i, 0))
    return pl.pallas_call(
        functools.partial(_ffn_kernel, layer=layer, final_norm=final_norm),
        grid=(n // tm,),
        in_specs=[row] + _ffn_weight_specs() + [_resident((1, D_MODEL))],
        out_specs=row,
        out_shape=jax.ShapeDtypeStruct((n, D_MODEL), F32),
        scratch_shapes=_ffn_scratch(tm),
        compiler_params=pltpu.CompilerParams(
            dimension_semantics=("arbitrary",), vmem_limit_bytes=VMEM_LIMIT),
        name="ffn",
    )(x, g, wg, wu, wd, fg)


def _windowed_gqa(q0, qa_ref, ka_ref, vat_ref, sink_ref, seq):
    tq = qa_ref.shape[0]
    mlo, mhi = _half_masks(BF16)
    b_idx = lax.broadcasted_iota(jnp.int32, (A_KEYS, A_BLOCK), 0)
    a_idx = lax.broadcasted_iota(jnp.int32, (A_KEYS, A_BLOCK), 1)
    row_blocks = []
    for r in range(tq // A_BLOCK):
        qs = q0 + r * A_BLOCK
        ws = pl.multiple_of(jnp.clip(qs - A_BLOCK, 0, seq - A_KEYS), A_BLOCK)
        dist = jnp.abs((qs - ws) + a_idx - b_idx)
        valid = dist <= WINDOW
        distf = dist.astype(F32)
        slabs = []
        for kv in range(A_KV_HEADS):
            kwin = ka_ref[pl.ds(ws, A_KEYS), kv * LANES:(kv + 1) * LANES]
            vwin = vat_ref[kv, :, pl.ds(ws, A_KEYS)]
            parts = []
            for g in range(A_GROUP):
                h = kv * A_GROUP + g
                slab = qa_ref[r * A_BLOCK:(r + 1) * A_BLOCK, (h // 2) * LANES:(h // 2 + 1) * LANES]
                parts.append(slab * (mlo if h % 2 == 0 else mhi))
            st = _dot_nt(kwin, jnp.concatenate(parts, axis=0))
            es, sinks = [], []
            for g in range(A_GROUP):
                h = kv * A_GROUP + g
                sg = st[:, g * A_BLOCK:(g + 1) * A_BLOCK]
                sg = jnp.where(valid, sg - SLOPES_A[h] * distf, NEG_INF)
                sk = sink_ref[h]
                m = jnp.maximum(jnp.max(sg, axis=0, keepdims=True), sk)
                es.append(jnp.exp((sg - m).astype(BF16)))
                sinks.append(jnp.exp(sk - m))
            ot = _dot(vwin, jnp.concatenate(es, axis=1))
            den = ot[HEAD_DIM:HEAD_DIM + 1, :] + jnp.concatenate(sinks, axis=1)
            o = ot[:HEAD_DIM, :] * (1.0 / den)
            for j in range(A_GROUP // 2):
                pair = jnp.concatenate([o[:, (2 * j) * A_BLOCK:(2 * j + 1) * A_BLOCK],
                                        o[:, (2 * j + 1) * A_BLOCK:(2 * j + 2) * A_BLOCK]], axis=0)
                slabs.append(pair.T)
        row_blocks.append(jnp.concatenate(slabs, axis=1))
    return jnp.concatenate(row_blocks, axis=0)


def _attn_kernel(x_ref, qa_ref, qb_ref, qaug_ref, ka_ref, vat_ref, kb_ref, vbt_ref, ksgn_ref, kaug_ref, fdt_ref,
                 wout_ref, sink_ref, lq1_ref, lk1_ref, lq2_ref, lk2_ref, sublnt_ref,
                 xo_ref, s0_scr, s1_scr, s2_scr, p0_scr, p1_scr, *, lam_init):
    tq = x_ref.shape[0]
    seq = kb_ref.shape[1]
    q0 = pl.multiple_of(pl.program_id(1) * tq, tq)
    diag = pl.ds(q0, tq)

    lam = (jnp.exp(jnp.sum(lq1_ref[...] * lk1_ref[...], axis=-1, keepdims=True))
           - jnp.exp(jnp.sum(lq2_ref[...] * lk2_ref[...], axis=-1, keepdims=True)) + lam_init)

    kaug_d = kaug_ref[diag, :]
    mlo, mhi = _half_masks(BF16)
    s_bufs = (s0_scr, s1_scr, s2_scr)
    p_bufs = (p0_scr, p1_scr)

    oa = _windowed_gqa(q0, qa_ref, ka_ref, vat_ref, sink_ref, seq)
    folded = oa[:, :LANES] + oa[:, LANES:2 * LANES] + oa[:, 2 * LANES:3 * LANES] + oa[:, 3 * LANES:]
    bits = lax.shift_right_logical(pltpu.bitcast(folded, jnp.uint32), jnp.uint32(16))
    anchor_zero = lax.shift_right_logical(bits, jnp.uint32(16)).astype(F32).astype(BF16)
    oa = oa.astype(BF16)

    def scores(h):
        s_scr = s_bufs[h % len(s_bufs)]
        q = qb_ref[h]
        if h == B_HEADS - 1:
            q = q + anchor_zero
        aq = qaug_ref[h]
        lhs_q = jnp.concatenate(
            [jnp.concatenate([q * mlo, q * mhi], axis=0), jnp.concatenate([aq, aq], axis=0)], axis=1)
        kcat = jnp.concatenate([kb_ref[h], ksgn_ref[...]], axis=1)
        st = _dot_nt(kcat, lhs_q)
        s_scr[...] = st
        m_far = jnp.max(st, axis=0, keepdims=True)
        kd = jnp.concatenate([kb_ref[h, diag, :], kaug_d], axis=1)
        sd = _dot_nt(kd, lhs_q) - SLOPES_B[h] * fdt_ref[...]
        s_scr[diag, :] = sd
        return jnp.maximum(m_far, jnp.max(sd, axis=0, keepdims=True))

    def softmax(h, m):
        s_scr, p_scr = s_bufs[h % len(s_bufs)], p_bufs[h % 2]
        for r in range(seq // EXP_ROWS):
            rows = slice(r * EXP_ROWS, (r + 1) * EXP_ROWS)
            p_scr[rows, :] = jnp.exp((s_scr[rows, :] - m).astype(BF16))

    def values(h):
        ot = _dot(vbt_ref[h], p_bufs[h % 2][...])
        ot = ot[:LANES, :] * (1.0 / ot[LANES:LANES + 1, :])
        o = ot[:, :tq] - lam * ot[:, tq:]
        ms = jnp.mean(o * o, axis=0, keepdims=True)
        y = o * lax.rsqrt(ms + RMS_EPS) * sublnt_ref[...] * (1.0 - lam_init)
        return y.T.astype(BF16)

    ob = [None] * B_HEADS
    m = [None] * B_HEADS
    for step in range(B_HEADS + 2):
        if step < B_HEADS:
            m[step] = scores(step)
        if 1 <= step <= B_HEADS:
            softmax(step - 1, m[step - 1])
        if step >= 2:
            ob[step - 2] = values(step - 2)

    mix = jnp.concatenate([oa] + ob, axis=1)
    xo_ref[...] = x_ref[...] + _dot(mix, wout_ref[...])


def _attn_call(x, qa, ka, vat, qb, kb, vbt, qaug, ksgn, kaug, fdt, wout, sink,
               lq1, lk1, lq2, lk2, sublnt, batch, seq, lam_init):
    n = x.shape[0]
    tq = TQ_ATTN
    nq = seq // tq
    once = pl.Buffered(1)
    tile = lambda cols: pl.BlockSpec((tq, cols), lambda b, i: (b * nq + i, 0))
    per_batch = lambda cols: pl.BlockSpec((seq, cols), lambda b, i: (b, 0), pipeline_mode=once)
    head_tile = pl.BlockSpec((B_HEADS, tq, LANES), lambda b, i: (0, b * nq + i, 0))
    smem = pl.BlockSpec(memory_space=pltpu.SMEM)
    small = lambda cols: _resident((1, cols))
    return pl.pallas_call(
        functools.partial(_attn_kernel, lam_init=lam_init),
        grid=(batch, nq),
        in_specs=[tile(D_MODEL), tile(A_Q_COLS), head_tile,
                  pl.BlockSpec((B_HEADS, tq, LANES), lambda b, i: (0, i, 0)),
                  per_batch(A_KV_DUP_COLS),
                  pl.BlockSpec((A_KV_HEADS, VAT_ROWS, seq), lambda b, i: (0, 0, b), pipeline_mode=once),
                  pl.BlockSpec((B_HEADS, seq, LANES), lambda b, i: (0, b, 0), pipeline_mode=once),
                  pl.BlockSpec((B_HEADS, VT_ROWS, seq), lambda b, i: (0, 0, b), pipeline_mode=once),
                  pl.BlockSpec((None, seq, LANES), lambda b, i: (i, 0, 0)),
                  _resident((seq, LANES)), _resident((tq, 2 * tq)), _resident((MIX_WIDTH, D_MODEL)), smem,
                  small(HEAD_DIM), small(HEAD_DIM), small(HEAD_DIM), small(HEAD_DIM), _resident((LANES, tq))],
        out_specs=tile(D_MODEL),
        out_shape=jax.ShapeDtypeStruct((n, D_MODEL), F32),
        scratch_shapes=[pltpu.VMEM((seq, 2 * tq), F32), pltpu.VMEM((seq, 2 * tq), F32),
                        pltpu.VMEM((seq, 2 * tq), F32),
                        pltpu.VMEM((seq, 2 * tq), BF16), pltpu.VMEM((seq, 2 * tq), BF16)],
        compiler_params=pltpu.CompilerParams(
            dimension_semantics=("arbitrary", "arbitrary"), vmem_limit_bytes=VMEM_LIMIT),
        name="attn",
    )(x, qa, qb, qaug, ka, vat, kb, vbt, ksgn, kaug, fdt, wout, sink, lq1, lk1, lq2, lk2, sublnt)


@functools.lru_cache(maxsize=None)
def _position_tables(seq, tq):
    f32, bf16 = np.float32, ml_dtypes.bfloat16
    pos = np.arange(seq, dtype=np.int32)
    lo = (pos % LANES).astype(f32)
    hi = (LANES * (pos // LANES)).astype(f32)
    one = np.ones((seq,), f32)
    pad = np.zeros((seq, LANES - 4), f32)
    kaug = np.concatenate([np.stack([lo, hi, one, one], axis=1), pad], axis=1)
    slopes = np.asarray(SLOPES_B, f32)[:, None, None]
    qcols = np.stack([one, one, -lo, -hi], axis=1)[None]
    assert DIAG_COL == 4
    qaug = np.concatenate([slopes * qcols, np.full((B_HEADS, seq, 1), DIAG_MASK, f32),
                           np.zeros((B_HEADS, seq, LANES - 5), f32)], axis=2)
    tile_of_key = (pos // tq)[None, :, None]
    tile = np.arange(seq // tq, dtype=np.int32)[:, None, None]
    flag = (np.arange(LANES) == DIAG_COL)[None, None, :] & (tile_of_key == tile)
    ksgn = np.where(tile_of_key <= tile, kaug[None], -kaug[None]) + flag.astype(f32)
    key = np.arange(tq, dtype=np.int32)[:, None]
    qry = np.arange(2 * tq, dtype=np.int32)[None, :] % tq
    fdt = (2 * np.maximum(key - qry, 0)).astype(f32)
    return qaug.astype(bf16), kaug.astype(bf16), ksgn.astype(bf16), fdt


def _prep_w_in(w):
    scale = HEAD_DIM ** -0.5
    c = 0
    qa = w[:, c:c + A_Q_COLS] * scale; c += A_Q_COLS
    ka = w[:, c:c + A_KV_HEADS * HEAD_DIM]; c += A_KV_HEADS * HEAD_DIM
    va = w[:, c:c + A_KV_HEADS * HEAD_DIM]; c += A_KV_HEADS * HEAD_DIM
    qb = w[:, c:c + B_COLS] * scale; c += B_COLS
    kb = w[:, c:c + B_COLS]; c += B_COLS
    vb = w[:, c:c + B_COLS]
    dup = lambda t: jnp.concatenate(
        [t[:, j * HEAD_DIM:(j + 1) * HEAD_DIM] for j in range(A_KV_HEADS) for _ in range(2)], axis=1)
    wvt = jnp.concatenate([vb, va], axis=1).T
    return jnp.concatenate([qa, dup(ka), qb, kb], axis=1).astype(BF16), wvt.astype(BF16)


def kernel(x, ffn1_norm, ffn1_w_gate, ffn1_w_up, ffn1_w_down, mix_norm, w_in, sink,
           lam_q1, lam_k1, lam_q2, lam_k2, diff_subln, w_out,
           ffn2_norm, ffn2_w_gate, ffn2_w_up, ffn2_w_down, final_norm):
    batch, seq, d = x.shape
    depth = w_in.shape[0]
    assert d == D_MODEL and seq % TQ_ATTN == 0 and (batch * seq) % TM_FFN == 0
    xf = x.reshape(batch * seq, d)
    qaug, kaug, ksgn, fdt = _position_tables(seq, TQ_ATTN)
    row = lambda v: v.reshape(1, -1).astype(F32)
    for l in range(depth):
        lam_init = 0.8 - 0.6 * math.exp(-0.3 * l)
        win, wvt = _prep_w_in(w_in[l])
        xf, qa, ka, vat, qb, kb, vbt = _ffn_inproj_call(
            xf, row(ffn1_norm[l]), ffn1_w_gate, ffn1_w_up, ffn1_w_down, row(mix_norm[l]), win, wvt, l)
        sublnt = jnp.broadcast_to(diff_subln[l].astype(F32)[:, None], (LANES, TQ_ATTN))
        xf = _attn_call(xf, qa, ka, vat, qb, kb, vbt, qaug, ksgn, kaug, fdt, w_out[l].astype(BF16),
                        sink[l].astype(F32), row(lam_q1[l]), row(lam_k1[l]),
                        row(lam_q2[l]), row(lam_k2[l]), sublnt, batch, seq, lam_init)
        xf = _ffn_call(xf, row(ffn2_norm[l]), ffn2_w_gate, ffn2_w_up, ffn2_w_down, row(final_norm),
                       l, final_norm=(l == depth - 1))
    return xf.reshape(batch, seq, d)
```

```python
import functools
import math

import jax
import jax.numpy as jnp
import ml_dtypes
import numpy as np
from jax import lax
from jax.experimental import pallas as pl
from jax.experimental.pallas import tpu as pltpu

F32 = jnp.float32
BF16 = jnp.bfloat16

D_MODEL = 1024
D_FF = 2816
HEAD_DIM = 64
WINDOW = 128
A_Q_HEADS = 8
A_KV_HEADS = 2
A_GROUP = A_Q_HEADS // A_KV_HEADS
B_HEADS = 4
RMS_EPS = 1e-6
NEG_INF = -1e30

LANES = 128
MXU_DIM = 256
BF16_ROWS = 16
VT_ROWS = LANES + BF16_ROWS
A_Q_COLS = A_Q_HEADS * HEAD_DIM
B_COLS = B_HEADS * LANES
A_KV_DUP_COLS = 2 * A_KV_HEADS * HEAD_DIM
IN_COLS_DUP = A_Q_COLS + A_KV_DUP_COLS + 2 * B_COLS
VT_COLS = B_COLS + A_KV_HEADS * HEAD_DIM
VAT_ROWS = HEAD_DIM + BF16_ROWS
MIX_WIDTH = A_Q_COLS + B_COLS

FF_CHUNK = MXU_DIM
TM_FFN = 512
W_STAGE_CHUNKS = 8
W_STAGE_SLOTS = 3
TQ_ATTN = 256
A_BLOCK = 128
A_KEYS = 3 * A_BLOCK
EXP_ROWS = 64
DIAG_COL = 4
DIAG_MASK = -(2.0 ** 100)
VMEM_LIMIT = 56 * 1024 * 1024
VMEM_LIMIT_ATTN = 62 * 1024 * 1024

SLOPES_A = tuple(2.0 ** (-8.0 * i / A_Q_HEADS) for i in range(1, A_Q_HEADS + 1))
SLOPES_B = tuple(2.0 ** (-8.0 * i / B_HEADS) for i in range(1, B_HEADS + 1))


def _rms(x, g):
    ms = jnp.mean(x * x, axis=-1, keepdims=True)
    return x * lax.rsqrt(ms + RMS_EPS) * g


def _dot(a, b):
    return jnp.dot(a, b, preferred_element_type=F32)


def _dot_nt(a, b):
    return lax.dot_general(a, b, (((1,), (1,)), ((), ())), preferred_element_type=F32)


def _half_masks(dtype):
    lane = lax.broadcasted_iota(jnp.int32, (1, LANES), 1)
    lo = (lane < HEAD_DIM).astype(F32).astype(dtype)
    hi = (lane >= HEAD_DIM).astype(F32).astype(dtype)
    return lo, hi


def _stage_bf16(w_hbm, layer, w_bf, stage, sem):
    slots, rows = stage.shape[0], stage.shape[1]
    n_chunks = w_bf.shape[0] // rows

    def copy(c):
        return pltpu.make_async_copy(
            w_hbm.at[layer, pl.ds(c * rows, rows), :], stage.at[c % slots], sem.at[c % slots])

    for c in range(min(slots - 1, n_chunks)):
        copy(c).start()
    for c in range(n_chunks):
        ahead = c + slots - 1
        if ahead < n_chunks:
            copy(ahead).start()
        copy(c).wait()
        w_bf[c * rows:(c + 1) * rows, :] = stage[c % slots].astype(BF16)


def _stage_ffn_weights(layer, wg_hbm, wu_hbm, wd_hbm, wg_bf, wu_bf, wd_bf, stage_up, stage_dn, sem):
    @pl.when(pl.program_id(0) == 0)
    def _():
        _stage_bf16(wg_hbm, layer, wg_bf, stage_up, sem)
        _stage_bf16(wu_hbm, layer, wu_bf, stage_up, sem)
        _stage_bf16(wd_hbm, layer, wd_bf, stage_dn, sem)


def _swiglu_residual(x, g_ref, wg_ref, wu_ref, wd_ref, h_scr):
    xn = _rms(x, g_ref[...]).astype(BF16)
    for c in range(D_FF // FF_CHUNK):
        sl = slice(c * FF_CHUNK, (c + 1) * FF_CHUNK)
        gate = _dot(xn, wg_ref[:, sl])
        up = _dot(xn, wu_ref[:, sl])
        h_scr[:, sl] = (gate * jax.nn.sigmoid(gate) * up).astype(BF16)
    return x + 0.5 * _dot(h_scr[...], wd_ref[...])


def _ffn_inproj_kernel(x_ref, g_ref, wg_hbm, wu_hbm, wd_hbm, mg_ref, win_ref, wvt_ref,
                       xo_ref, qa_ref, ka_ref, vat_ref, qb_ref, kb_ref, vbt_ref,
                       h_scr, wg_ref, wu_ref, wd_ref, stage_up, stage_dn, sem, *, layer):
    _stage_ffn_weights(layer, wg_hbm, wu_hbm, wd_hbm, wg_ref, wu_ref, wd_ref, stage_up, stage_dn, sem)
    x1 = _swiglu_residual(x_ref[...], g_ref, wg_ref, wu_ref, wd_ref, h_scr)
    xo_ref[...] = x1
    hn = _rms(x1, mg_ref[...]).astype(BF16)
    proj = _dot(hn, win_ref[...]).astype(BF16)
    c0 = A_Q_COLS
    c2 = c0 + A_KV_DUP_COLS
    qa_ref[...] = proj[:, :c0]
    ka_ref[...] = proj[:, c0:c2]
    vt = _dot_nt(wvt_ref[...], hn).astype(BF16)
    ones = jnp.ones((BF16_ROWS, vt.shape[1]), BF16)
    for kv in range(A_KV_HEADS):
        vat_ref[kv, :HEAD_DIM, :] = vt[B_COLS + kv * HEAD_DIM:B_COLS + (kv + 1) * HEAD_DIM, :]
        vat_ref[kv, HEAD_DIM:, :] = ones
    for h in range(B_HEADS):
        qb_ref[h] = proj[:, c2 + h * LANES: c2 + (h + 1) * LANES]
        kb_ref[h] = proj[:, c2 + B_COLS + h * LANES: c2 + B_COLS + (h + 1) * LANES]
        vbt_ref[h, :LANES, :] = vt[h * LANES:(h + 1) * LANES, :]
        vbt_ref[h, LANES:, :] = ones


def _ffn_kernel(x_ref, g_ref, wg_hbm, wu_hbm, wd_hbm, fg_ref, xo_ref,
                h_scr, wg_ref, wu_ref, wd_ref, stage_up, stage_dn, sem, *, layer, final_norm):
    _stage_ffn_weights(layer, wg_hbm, wu_hbm, wd_hbm, wg_ref, wu_ref, wd_ref, stage_up, stage_dn, sem)
    x1 = _swiglu_residual(x_ref[...], g_ref, wg_ref, wu_ref, wd_ref, h_scr)
    if final_norm:
        x1 = _rms(x1, fg_ref[...])
    xo_ref[...] = x1


def _resident(shape):
    nd = len(shape)
    return pl.BlockSpec(shape, lambda *_: (0,) * nd, pipeline_mode=pl.Buffered(1))


def _ffn_weight_specs():
    hbm = pl.BlockSpec(memory_space=pl.ANY)
    return [_resident((1, D_MODEL)), hbm, hbm, hbm]


def _ffn_scratch(tm):
    up_rows = D_MODEL // W_STAGE_CHUNKS
    dn_rows = D_FF // W_STAGE_CHUNKS
    return [pltpu.VMEM((tm, D_FF), BF16),
            pltpu.VMEM((D_MODEL, D_FF), BF16), pltpu.VMEM((D_MODEL, D_FF), BF16), pltpu.VMEM((D_FF, D_MODEL), BF16),
            pltpu.VMEM((W_STAGE_SLOTS, up_rows, D_FF), F32), pltpu.VMEM((W_STAGE_SLOTS, dn_rows, D_MODEL), F32),
            pltpu.SemaphoreType.DMA((W_STAGE_SLOTS,))]


def _ffn_inproj_call(x, g, wg, wu, wd, mg, win, wvt, layer):
    n = x.shape[0]
    tm = TM_FFN
    row = lambda cols: pl.BlockSpec((tm, cols), lambda i: (i, 0))
    headrow = pl.BlockSpec((B_HEADS, tm, LANES), lambda i: (0, i, 0))
    headcol = pl.BlockSpec((B_HEADS, VT_ROWS, tm), lambda i: (0, 0, i))
    kvcol = pl.BlockSpec((A_KV_HEADS, VAT_ROWS, tm), lambda i: (0, 0, i))
    out_shape = (
        jax.ShapeDtypeStruct((n, D_MODEL), F32),
        jax.ShapeDtypeStruct((n, A_Q_COLS), BF16),
        jax.ShapeDtypeStruct((n, A_KV_DUP_COLS), BF16),
        jax.ShapeDtypeStruct((A_KV_HEADS, VAT_ROWS, n), BF16),
        jax.ShapeDtypeStruct((B_HEADS, n, LANES), BF16),
        jax.ShapeDtypeStruct((B_HEADS, n, LANES), BF16),
        jax.ShapeDtypeStruct((B_HEADS, VT_ROWS, n), BF16),
    )
    return pl.pallas_call(
        functools.partial(_ffn_inproj_kernel, layer=layer),
        grid=(n // tm,),
        in_specs=[row(D_MODEL)] + _ffn_weight_specs()
                 + [_resident((1, D_MODEL)), _resident((D_MODEL, IN_COLS_DUP)), _resident((VT_COLS, D_MODEL))],
        out_specs=(row(D_MODEL), row(A_Q_COLS), row(A_KV_DUP_COLS), kvcol,
                   headrow, headrow, headcol),
        out_shape=out_shape,
        scratch_shapes=_ffn_scratch(tm),
        compiler_params=pltpu.CompilerParams(
            dimension_semantics=("arbitrary",), vmem_limit_bytes=VMEM_LIMIT),
        name="ffn_inproj",
    )(x, g, wg, wu, wd, mg, win, wvt)


def _ffn_call(x, g, wg, wu, wd, fg, layer, final_norm):
    n = x.shape[0]
    tm = TM_FFN
    row = pl.BlockSpec((tm, D_MODEL), lambda i: (i, 0))
    return pl.pallas_call(
        functools.partial(_ffn_kernel, layer=layer, final_norm=final_norm),
        grid=(n // tm,),
        in_specs=[row] + _ffn_weight_specs() + [_resident((1, D_MODEL))],
        out_specs=row,
        out_shape=jax.ShapeDtypeStruct((n, D_MODEL), F32),
        scratch_shapes=_ffn_scratch(tm),
        compiler_params=pltpu.CompilerParams(
            dimension_semantics=("arbitrary",), vmem_limit_bytes=VMEM_LIMIT),
        name="ffn",
    )(x, g, wg, wu, wd, fg)


def _windowed_gqa(q0, qa_ref, ka_ref, vat_ref, sink_ref, seq):
    tq = qa_ref.shape[0]
    mlo, mhi = _half_masks(BF16)
    b_idx = lax.broadcasted_iota(jnp.int32, (A_KEYS, A_BLOCK), 0)
    a_idx = lax.broadcasted_iota(jnp.int32, (A_KEYS, A_BLOCK), 1)
    row_blocks = []
    for r in range(tq // A_BLOCK):
        qs = q0 + r * A_BLOCK
        ws = pl.multiple_of(jnp.clip(qs - A_BLOCK, 0, seq - A_KEYS), A_BLOCK)
        dist = jnp.abs((qs - ws) + a_idx - b_idx)
        valid = dist <= WINDOW
        distf = dist.astype(F32)
        slabs = []
        for kv in range(A_KV_HEADS):
            kwin = ka_ref[pl.ds(ws, A_KEYS), kv * LANES:(kv + 1) * LANES]
            vwin = vat_ref[kv, :, pl.ds(ws, A_KEYS)]
            parts = []
            for g in range(A_GROUP):
                h = kv * A_GROUP + g
                slab = qa_ref[r * A_BLOCK:(r + 1) * A_BLOCK, (h // 2) * LANES:(h // 2 + 1) * LANES]
                parts.append(slab * (mlo if h % 2 == 0 else mhi))
            st = _dot_nt(kwin, jnp.concatenate(parts, axis=0))
            es, sinks = [], []
            for g in range(A_GROUP):
                h = kv * A_GROUP + g
                sg = st[:, g * A_BLOCK:(g + 1) * A_BLOCK]
                sg = jnp.where(valid, sg - SLOPES_A[h] * distf, NEG_INF)
                sk = sink_ref[h]
                m = jnp.maximum(jnp.max(sg, axis=0, keepdims=True), sk)
                es.append(jnp.exp((sg - m).astype(BF16)))
                sinks.append(jnp.exp(sk - m))
            ot = _dot(vwin, jnp.concatenate(es, axis=1))
            den = ot[HEAD_DIM:HEAD_DIM + 1, :] + jnp.concatenate(sinks, axis=1)
            o = ot[:HEAD_DIM, :] * (1.0 / den)
            for j in range(A_GROUP // 2):
                pair = jnp.concatenate([o[:, (2 * j) * A_BLOCK:(2 * j + 1) * A_BLOCK],
                                        o[:, (2 * j + 1) * A_BLOCK:(2 * j + 2) * A_BLOCK]], axis=0)
                slabs.append(pair.T)
        row_blocks.append(jnp.concatenate(slabs, axis=1))
    return jnp.concatenate(row_blocks, axis=0)


def _attn_kernel(x_ref, qa_ref, qb_ref, qaug_ref, ka_ref, vat_ref, kb_ref, vbt_ref, ksgn_ref, kaug_ref, fdt_ref,
                 wout_ref, sink_ref, lq1_ref, lk1_ref, lq2_ref, lk2_ref, sublnt_ref,
                 xo_ref, s0_scr, s1_scr, s2_scr, p0_scr, p1_scr, *, lam_init):
    tq = x_ref.shape[0]
    seq = kb_ref.shape[1]
    q0 = pl.multiple_of(pl.program_id(1) * tq, tq)
    diag = pl.ds(q0, tq)

    lam = (jnp.exp(jnp.sum(lq1_ref[...] * lk1_ref[...], axis=-1, keepdims=True))
           - jnp.exp(jnp.sum(lq2_ref[...] * lk2_ref[...], axis=-1, keepdims=True)) + lam_init)

    kaug_d = kaug_ref[diag, :]
    mlo, mhi = _half_masks(BF16)
    s_bufs = (s0_scr, s1_scr, s2_scr)
    p_bufs = (p0_scr, p1_scr)

    oa = _windowed_gqa(q0, qa_ref, ka_ref, vat_ref, sink_ref, seq)
    folded = oa[:, :LANES] + oa[:, LANES:2 * LANES] + oa[:, 2 * LANES:3 * LANES] + oa[:, 3 * LANES:]
    bits = lax.shift_right_logical(pltpu.bitcast(folded, jnp.uint32), jnp.uint32(16))
    anchor_zero = lax.shift_right_logical(bits, jnp.uint32(16)).astype(F32).astype(BF16)
    oa = oa.astype(BF16)

    def scores(h):
        s_scr = s_bufs[h % len(s_bufs)]
        q = qb_ref[h]
        if h == B_HEADS - 1:
            q = q + anchor_zero
        aq = qaug_ref[h]
        lhs_q = jnp.concatenate(
            [jnp.concatenate([q * mlo, q * mhi], axis=0), jnp.concatenate([aq, aq], axis=0)], axis=1)
        kcat = jnp.concatenate([kb_ref[h], ksgn_ref[...]], axis=1)
        st = _dot_nt(kcat, lhs_q)
        s_scr[...] = st
        m_far = jnp.max(st, axis=0, keepdims=True)
        kd = jnp.concatenate([kb_ref[h, diag, :], kaug_d], axis=1)
        sd = _dot_nt(kd, lhs_q) - SLOPES_B[h] * fdt_ref[...]
        s_scr[diag, :] = sd
        return jnp.maximum(m_far, jnp.max(sd, axis=0, keepdims=True))

    def softmax(h, m):
        s_scr, p_scr = s_bufs[h % len(s_bufs)], p_bufs[h % 2]
        for r in range(seq // EXP_ROWS):
            rows = slice(r * EXP_ROWS, (r + 1) * EXP_ROWS)
            p_scr[rows, :] = jnp.exp((s_scr[rows, :] - m).astype(BF16))

    def values(h):
        ot = _dot(vbt_ref[h], p_bufs[h % 2][...])
        ot = ot[:LANES, :] * (1.0 / ot[LANES:LANES + 1, :])
        o = ot[:, :tq] - lam * ot[:, tq:]
        ms = jnp.mean(o * o, axis=0, keepdims=True)
        y = o * lax.rsqrt(ms + RMS_EPS) * sublnt_ref[...] * (1.0 - lam_init)
        return y.T.astype(BF16)

    ob = [None] * B_HEADS
    m = [None] * B_HEADS
    for step in range(B_HEADS + 2):
        if step < B_HEADS:
            m[step] = scores(step)
        if 1 <= step <= B_HEADS:
            softmax(step - 1, m[step - 1])
        if step >= 2:
            ob[step - 2] = values(step - 2)

    mix = jnp.concatenate([oa] + ob, axis=1)
    xo_ref[...] = x_ref[...] + _dot(mix, wout_ref[...])


def _attn_call(x, qa, ka, vat, qb, kb, vbt, qaug, ksgn, kaug, fdt, wout, sink,
               lq1, lk1, lq2, lk2, sublnt, batch, seq, lam_init):
    n = x.shape[0]
    tq = TQ_ATTN
    nq = seq // tq
    once = pl.Buffered(1)
    tile = lambda cols: pl.BlockSpec((tq, cols), lambda b, i: (b * nq + i, 0))
    per_batch = lambda cols: pl.BlockSpec((seq, cols), lambda b, i: (b, 0), pipeline_mode=once)
    head_tile = pl.BlockSpec((B_HEADS, tq, LANES), lambda b, i: (0, b * nq + i, 0))
    smem = pl.BlockSpec(memory_space=pltpu.SMEM)
    small = lambda cols: _resident((1, cols))
    return pl.pallas_call(
        functools.partial(_attn_kernel, lam_init=lam_init),
        grid=(batch, nq),
        in_specs=[tile(D_MODEL), tile(A_Q_COLS), head_tile,
                  pl.BlockSpec((B_HEADS, tq, LANES), lambda b, i: (0, i, 0)),
                  per_batch(A_KV_DUP_COLS),
                  pl.BlockSpec((A_KV_HEADS, VAT_ROWS, seq), lambda b, i: (0, 0, b), pipeline_mode=once),
                  pl.BlockSpec((B_HEADS, seq, LANES), lambda b, i: (0, b, 0)),
                  pl.BlockSpec((B_HEADS, VT_ROWS, seq), lambda b, i: (0, 0, b)),
                  pl.BlockSpec((None, seq, LANES), lambda b, i: (i, 0, 0)),
                  _resident((seq, LANES)), _resident((tq, 2 * tq)), _resident((MIX_WIDTH, D_MODEL)), smem,
                  small(HEAD_DIM), small(HEAD_DIM), small(HEAD_DIM), small(HEAD_DIM), _resident((LANES, tq))],
        out_specs=tile(D_MODEL),
        out_shape=jax.ShapeDtypeStruct((n, D_MODEL), F32),
        scratch_shapes=[pltpu.VMEM((seq, 2 * tq), F32), pltpu.VMEM((seq, 2 * tq), F32),
                        pltpu.VMEM((seq, 2 * tq), F32),
                        pltpu.VMEM((seq, 2 * tq), BF16), pltpu.VMEM((seq, 2 * tq), BF16)],
        compiler_params=pltpu.CompilerParams(
            dimension_semantics=("arbitrary", "arbitrary"), vmem_limit_bytes=VMEM_LIMIT_ATTN),
        name="attn",
    )(x, qa, qb, qaug, ka, vat, kb, vbt, ksgn, kaug, fdt, wout, sink, lq1, lk1, lq2, lk2, sublnt)


@functools.lru_cache(maxsize=None)
def _position_tables(seq, tq):
    f32, bf16 = np.float32, ml_dtypes.bfloat16
    pos = np.arange(seq, dtype=np.int32)
    lo = (pos % LANES).astype(f32)
    hi = (LANES * (pos // LANES)).astype(f32)
    one = np.ones((seq,), f32)
    pad = np.zeros((seq, LANES - 4), f32)
    kaug = np.concatenate([np.stack([lo, hi, one, one], axis=1), pad], axis=1)
    slopes = np.asarray(SLOPES_B, f32)[:, None, None]
    qcols = np.stack([one, one, -lo, -hi], axis=1)[None]
    assert DIAG_COL == 4
    qaug = np.concatenate([slopes * qcols, np.full((B_HEADS, seq, 1), DIAG_MASK, f32),
                           np.zeros((B_HEADS, seq, LANES - 5), f32)], axis=2)
    tile_of_key = (pos // tq)[None, :, None]
    tile = np.arange(seq // tq, dtype=np.int32)[:, None, None]
    flag = (np.arange(LANES) == DIAG_COL)[None, None, :] & (tile_of_key == tile)
    ksgn = np.where(tile_of_key <= tile, kaug[None], -kaug[None]) + flag.astype(f32)
    key = np.arange(tq, dtype=np.int32)[:, None]
    qry = np.arange(2 * tq, dtype=np.int32)[None, :] % tq
    fdt = (2 * np.maximum(key - qry, 0)).astype(f32)
    return qaug.astype(bf16), kaug.astype(bf16), ksgn.astype(bf16), fdt


def _prep_w_in(w):
    scale = HEAD_DIM ** -0.5
    c = 0
    qa = w[:, c:c + A_Q_COLS] * scale; c += A_Q_COLS
    ka = w[:, c:c + A_KV_HEADS * HEAD_DIM]; c += A_KV_HEADS * HEAD_DIM
    va = w[:, c:c + A_KV_HEADS * HEAD_DIM]; c += A_KV_HEADS * HEAD_DIM
    qb = w[:, c:c + B_COLS] * scale; c += B_COLS
    kb = w[:, c:c + B_COLS]; c += B_COLS
    vb = w[:, c:c + B_COLS]
    dup = lambda t: jnp.concatenate(
        [t[:, j * HEAD_DIM:(j + 1) * HEAD_DIM] for j in range(A_KV_HEADS) for _ in range(2)], axis=1)
    wvt = jnp.concatenate([vb, va], axis=1).T
    return jnp.concatenate([qa, dup(ka), qb, kb], axis=1).astype(BF16), wvt.astype(BF16)


def kernel(x, ffn1_norm, ffn1_w_gate, ffn1_w_up, ffn1_w_down, mix_norm, w_in, sink,
           lam_q1, lam_k1, lam_q2, lam_k2, diff_subln, w_out,
           ffn2_norm, ffn2_w_gate, ffn2_w_up, ffn2_w_down, final_norm):
    batch, seq, d = x.shape
    depth = w_in.shape[0]
    assert d == D_MODEL and seq % TQ_ATTN == 0 and (batch * seq) % TM_FFN == 0
    xf = x.reshape(batch * seq, d)
    qaug, kaug, ksgn, fdt = _position_tables(seq, TQ_ATTN)
    row = lambda v: v.reshape(1, -1).astype(F32)
    for l in range(depth):
        lam_init = 0.8 - 0.6 * math.exp(-0.3 * l)
        win, wvt = _prep_w_in(w_in[l])
        xf, qa, ka, vat, qb, kb, vbt = _ffn_inproj_call(
            xf, row(ffn1_norm[l]), ffn1_w_gate, ffn1_w_up, ffn1_w_down, row(mix_norm[l]), win, wvt, l)
        sublnt = jnp.broadcast_to(diff_subln[l].astype(F32)[:, None], (LANES, TQ_ATTN))
        xf = _attn_call(xf, qa, ka, vat, qb, kb, vbt, qaug, ksgn, kaug, fdt, w_out[l].astype(BF16),
                        sink[l].astype(F32), row(lam_q1[l]), row(lam_k1[l]),
                        row(lam_q2[l]), row(lam_k2[l]), sublnt, batch, seq, lam_init)
        xf = _ffn_call(xf, row(ffn2_norm[l]), ffn2_w_gate, ffn2_w_up, ffn2_w_down, row(final_norm),
                       l, final_norm=(l == depth - 1))
    return xf.reshape(batch, seq, d)
```
